```python
import math
import jax, jax.numpy as jnp
from jax import lax
import numpy as np

D_MODEL = 1024
BATCH = 4
SEQ = 8192
DEPTH = 2

N_MEM = 256
GROUP_WIDTH = 512
N_GROUPS = 3
MIX_WIDTH = N_GROUPS * GROUP_WIDTH
ATT_HEADS = 8
ATT_HEAD_DIM = GROUP_WIDTH // ATT_HEADS
MOBA_BLOCK = 256
MOBA_TOPK = 3
Q_CHUNK = 128
LRU_HEADS = 8
LRU_HEAD_DIM = GROUP_WIDTH // LRU_HEADS
LRU_CONV = 4
LRU_C = 8.0
SC_CONV = 3
XATTN_HEADS = 4
XATTN_HEAD_DIM = D_MODEL // XATTN_HEADS
N_IN_SPLITS = 10
IN_COLS = N_IN_SPLITS * GROUP_WIDTH
DN_ALPHA = (2.0 * DEPTH) ** 0.25
DN_BETA = (8.0 * DEPTH) ** -0.25
LN_EPS = 1e-5
RMS_EPS = 1e-6

kernel_name = 'hymba_moba_rglru_shortconv_deepnorm'


def layer_norm(x, g, b):
    xf = x.astype(jnp.float32)
    mu = jnp.mean(xf, axis=-1, keepdims=True)
    var = jnp.mean(jnp.square(xf - mu), axis=-1, keepdims=True)
    return ((xf - mu) * lax.rsqrt(var + LN_EPS) * g.astype(jnp.float32) + b.astype(jnp.float32)).astype(x.dtype)


def group_rms_norm(y, g):
    yf = y.astype(jnp.float32)
    r = lax.rsqrt(jnp.mean(yf * yf, axis=-1, keepdims=True) + RMS_EPS)
    return (yf * r * g.astype(jnp.float32)).astype(y.dtype)


def causal_dwconv(x, w):
    k, c = w.shape
    return lax.conv_general_dilated(
        x, w[:, None, :].astype(x.dtype), window_strides=(1,), padding=[(k - 1, 0)],
        dimension_numbers=('NWC', 'WIO', 'NWC'), feature_group_count=c)


def moba_attention(q, k, v):
    b, h, s, dh = q.shape
    n_blk = -(-s // MOBA_BLOCK)
    s_pad = n_blk * MOBA_BLOCK
    pad = ((0, 0), (0, 0), (0, s_pad - s), (0, 0))
    k_blocks = jnp.pad(k, pad).reshape(b, h, n_blk, MOBA_BLOCK, dh)
    v_blocks = jnp.pad(v, pad).reshape(b, h, n_blk, MOBA_BLOCK, dh)
    k_mean = jnp.mean(k_blocks.astype(jnp.float32), axis=3).astype(k.dtype)
    n_sel = min(MOBA_TOPK, n_blk)
    scale = dh ** -0.5
    blk_ids = jnp.arange(n_blk)
    gather = jax.vmap(jax.vmap(lambda blocks, idx: blocks[idx]))

    def chunk(ci):
        start = ci * Q_CHUNK
        qc = lax.dynamic_slice_in_dim(q, start, Q_CHUNK, axis=2)
        own = start // MOBA_BLOCK
        q_pos = start + jnp.arange(Q_CHUNK)
        gate = jnp.einsum('bhqd,bhnd->bhqn', qc, k_mean).astype(jnp.float32)
        gate = jnp.where(blk_ids < own, gate, -jnp.inf)
        _, sel = lax.top_k(gate, n_sel)
        sel_valid = jnp.arange(n_sel) < own
        k_sel = gather(k_blocks, sel)
        v_sel = gather(v_blocks, sel)
        k_own = lax.dynamic_slice_in_dim(k_blocks, own, 1, axis=2)[:, :, 0]
        v_own = lax.dynamic_slice_in_dim(v_blocks, own, 1, axis=2)[:, :, 0]
        s_sel = jnp.einsum('bhqd,bhqnkd->bhqnk', qc, k_sel).astype(jnp.float32) * scale
        s_sel = jnp.where(sel_valid[:, None], s_sel, -jnp.inf)
        s_own = jnp.einsum('bhqd,bhkd->bhqk', qc, k_own).astype(jnp.float32) * scale
        k_pos = own * MOBA_BLOCK + jnp.arange(MOBA_BLOCK)
        s_own = jnp.where(k_pos[None, :] <= q_pos[:, None], s_own, -jnp.inf)
        n_sk = n_sel * MOBA_BLOCK
        scores = jnp.concatenate([s_sel.reshape(b, h, Q_CHUNK, n_sk), s_own], axis=-1)
        p = jax.nn.softmax(scores, axis=-1)
        p_sel = p[..., :n_sk].reshape(b, h, Q_CHUNK, n_sel, MOBA_BLOCK).astype(v.dtype)
        p_own = p[..., n_sk:].astype(v.dtype)
        return (jnp.einsum('bhqnk,bhqnkd->bhqd', p_sel, v_sel)
                + jnp.einsum('bhqk,bhkd->bhqd', p_own, v_own))

    outs = lax.map(chunk, jnp.arange(s // Q_CHUNK))
    return outs.transpose(1, 2, 0, 3, 4).reshape(b, h, s, dh)


def rg_lru(x, w_a, b_a, w_x, b_x, lam):
    b, s, w = x.shape
    xh = x.reshape(b, s, LRU_HEADS, LRU_HEAD_DIM)
    r = jax.nn.sigmoid(jnp.einsum('bshi,hij->bshj', xh, w_a).reshape(b, s, w) + b_a)
    i = jax.nn.sigmoid(jnp.einsum('bshi,hij->bshj', xh, w_x).reshape(b, s, w) + b_x)
    log_a = -LRU_C * r.astype(jnp.float32) * jax.nn.softplus(-lam.astype(jnp.float32))
    a = jnp.exp(log_a)
    u = jnp.sqrt(-jnp.expm1(2.0 * log_a)) * (i * x).astype(jnp.float32)

    def combine(c1, c2):
        a1, b1 = c1
        a2, b2 = c2
        return a1 * a2, a2 * b1 + b2

    _, hs = lax.associative_scan(combine, (a, u), axis=1)
    return hs.astype(x.dtype)


def hybrid_mixer(x, w_in, lru_conv_w, lru_conv_b, lru_w_a, lru_b_a, lru_w_x, lru_b_x,
                 lru_lambda, sc_conv_w, group_gain, w_out):
    b, s, _ = x.shape
    proj = x @ w_in
    q, k, v, g_att, x_lru, g_lru, sc_b, sc_c, sc_x, g_sc = jnp.split(proj, N_IN_SPLITS, axis=-1)
    heads = lambda t: t.reshape(b, s, ATT_HEADS, ATT_HEAD_DIM).transpose(0, 2, 1, 3)
    y_att = moba_attention(heads(q), heads(k), heads(v)).transpose(0, 2, 1, 3).reshape(b, s, GROUP_WIDTH)
    xl = causal_dwconv(x_lru, lru_conv_w) + lru_conv_b
    y_lru = rg_lru(xl, lru_w_a, lru_b_a, lru_w_x, lru_b_x, lru_lambda)
    y_sc = sc_b * causal_dwconv(sc_c * sc_x, sc_conv_w)
    y = jnp.stack([y_att, y_lru, y_sc], axis=2)
    gates = jnp.stack([g_att, g_lru, g_sc], axis=2)
    y = group_rms_norm(y, group_gain) * jax.nn.silu(gates)
    return y.reshape(b, s, MIX_WIDTH) @ w_out


def memory_cross_attention(x, mem, wq, wk, wv, wo):
    b, s, _ = x.shape
    m = mem.shape[1]
    q = (x @ wq).reshape(b, s, XATTN_HEADS, XATTN_HEAD_DIM)
    k = (mem @ wk).reshape(b, m, XATTN_HEADS, XATTN_HEAD_DIM)
    v = (mem @ wv).reshape(b, m, XATTN_HEADS, XATTN_HEAD_DIM)
    scores = jnp.einsum('bshd,bmhd->bhsm', q, k).astype(jnp.float32) * (XATTN_HEAD_DIM ** -0.5)
    p = jax.nn.softmax(scores, axis=-1).astype(v.dtype)
    o = jnp.einsum('bhsm,bmhd->bshd', p, v).reshape(b, s, D_MODEL)
    return o @ wo


def setup_inputs(seed: int = 0) -> dict:
    key = jax.random.key(seed)
    ks = jax.random.split(key, 24)
    f32 = jnp.float32
    nrm = lambda k, shape, scale: jax.random.normal(k, shape, f32) * scale
    L = DEPTH
    G = GROUP_WIDTH
    u = jax.random.uniform(ks[9], (L, G), f32, 0.9, 0.999)
    p = u ** (1.0 / LRU_C)
    return {
        'x': nrm(ks[0], (BATCH, SEQ, D_MODEL), 1.0),
        'mem': nrm(ks[1], (BATCH, N_MEM, D_MODEL), 1.0),
        'w_in': nrm(ks[2], (L, D_MODEL, IN_COLS), D_MODEL ** -0.5),
        'lru_conv_w': nrm(ks[3], (L, LRU_CONV, G), LRU_CONV ** -0.5),
        'lru_conv_b': nrm(ks[4], (L, G), 0.02),
        'lru_w_a': nrm(ks[5], (L, LRU_HEADS, LRU_HEAD_DIM, LRU_HEAD_DIM), LRU_HEAD_DIM ** -0.5),
        'lru_b_a': nrm(ks[6], (L, G), 0.02),
        'lru_w_x': nrm(ks[7], (L, LRU_HEADS, LRU_HEAD_DIM, LRU_HEAD_DIM), LRU_HEAD_DIM ** -0.5),
        'lru_b_x': nrm(ks[8], (L, G), 0.02),
        'lru_lambda': jnp.log(p) - jnp.log1p(-p),
        'sc_conv_w': nrm(ks[10], (L, SC_CONV, G), SC_CONV ** -0.5),
        'group_gain': 1.0 + nrm(ks[11], (L, N_GROUPS, G), 0.02),
        'w_out': nrm(ks[12], (L, MIX_WIDTH, D_MODEL), MIX_WIDTH ** -0.5 * DN_BETA),
        'ln1_g': 1.0 + nrm(ks[13], (L, D_MODEL), 0.02),
        'ln1_b': nrm(ks[14], (L, D_MODEL), 0.02),
        'xq_w': nrm(ks[15], (L, D_MODEL, D_MODEL), D_MODEL ** -0.5),
        'xk_w': nrm(ks[16], (L, D_MODEL, D_MODEL), D_MODEL ** -0.5),
        'xv_w': nrm(ks[17], (L, D_MODEL, D_MODEL), D_MODEL ** -0.5),
        'xo_w': nrm(ks[18], (L, D_MODEL, D_MODEL), D_MODEL ** -0.5 * DN_BETA),
        'ln2_g': 1.0 + nrm(ks[19], (L, D_MODEL), 0.02),
        'ln2_b': nrm(ks[20], (L, D_MODEL), 0.02),
    }


def reference(x, mem, w_in, lru_conv_w, lru_conv_b, lru_w_a, lru_b_a, lru_w_x, lru_b_x,
              lru_lambda, sc_conv_w, group_gain, w_out, ln1_g, ln1_b,
              xq_w, xk_w, xv_w, xo_w, ln2_g, ln2_b):
    for l in range(DEPTH):
        mix = hybrid_mixer(x, w_in[l], lru_conv_w[l], lru_conv_b[l], lru_w_a[l], lru_b_a[l],
                           lru_w_x[l], lru_b_x[l], lru_lambda[l], sc_conv_w[l], group_gain[l], w_out[l])
        x = layer_norm(DN_ALPHA * x + mix, ln1_g[l], ln1_b[l])
        xat = memory_cross_attention(x, mem, xq_w[l], xk_w[l], xv_w[l], xo_w[l])
        x = layer_norm(DN_ALPHA * x + xat, ln2_g[l], ln2_b[l])
    return x
```

```python
import functools

import jax
import jax.numpy as jnp
from jax import lax
from jax.experimental import pallas as pl
from jax.experimental.pallas import tpu as pltpu

F32 = jnp.float32
_MXU_DTYPE = jnp.bfloat16

GROUP_WIDTH = 512
ATT_HEAD_DIM = 64
MOBA_BLOCK = 256
MOBA_TOPK = 3
LRU_HEADS = 8
LRU_C = 8.0
XATTN_HEADS = 4
DEPTH = 2
DN_ALPHA = (2.0 * DEPTH) ** 0.25
LN_EPS = 1e-5
RMS_EPS = 1e-6

LANES = 128
SUBLANES = 8
VMEM_LIMIT = 48 * 1024 * 1024
MASK_BIAS = -1e30


def _dot(a, b):
    return jnp.dot(a, b, preferred_element_type=F32)


def _dot_nt(a, b):
    return lax.dot_general(a, b, (((1,), (1,)), ((), ())), preferred_element_type=F32)


def _sigmoid(x):
    return 1.0 / (1.0 + jnp.exp(-x))


def _matmul_kernel(x_ref, w_ref, o_ref, *, n_chunk):
    xb = x_ref[...].astype(_MXU_DTYPE)
    n = w_ref.shape[1]
    for c in range(0, n, n_chunk):
        o_ref[:, c:c + n_chunk] = _dot(xb, w_ref[:, c:c + n_chunk]).astype(o_ref.dtype)


def _matmul(x, w, out_dtype, tm, n_chunk=512):
    m, k = x.shape
    n = w.shape[1]
    assert m % tm == 0 and n % n_chunk == 0
    return pl.pallas_call(
        functools.partial(_matmul_kernel, n_chunk=n_chunk),
        grid=(m // tm,),
        in_specs=[pl.BlockSpec((tm, k), lambda i: (i, 0)),
                  pl.BlockSpec((k, n), lambda i: (0, 0))],
        out_specs=pl.BlockSpec((tm, n), lambda i: (i, 0)),
        out_shape=jax.ShapeDtypeStruct((m, n), out_dtype),
        compiler_params=pltpu.CompilerParams(dimension_semantics=("arbitrary",),
                                             vmem_limit_bytes=VMEM_LIMIT),
    )(x, w)


def _moba_kernel(q_ref, k_ref, v_ref, o_ref, kmean_ref, m_ref, l_ref, acc_ref, *, blk, dh, n_blk):
    qi = pl.program_id(2)
    lane = lax.broadcasted_iota(jnp.int32, (blk, LANES), 1)

    @pl.when(qi == 0)
    def _():
        kmean_ref[...] = jnp.zeros_like(kmean_ref)

        def mean_body(j, c):
            kb = k_ref[pl.ds(pl.multiple_of(j * blk, blk), blk), :].astype(F32)
            kmean_ref[pl.ds(j, 1), :] = jnp.sum(kb, axis=0, keepdims=True) * (1.0 / blk)
            return c

        lax.fori_loop(0, n_blk, mean_body, 0)

    scale = dh ** -0.5
    qs = q_ref[...] * jnp.asarray(scale, q_ref.dtype)
    head0 = lane < dh
    zero = jnp.zeros_like(qs)
    q0 = jnp.where(head0, qs, zero)
    q1 = jnp.where(head0, zero, qs)

    km = kmean_ref[...]
    km_hi = km.astype(_MXU_DTYPE)
    km_lo = (km - km_hi.astype(F32)).astype(_MXU_DTYPE)
    lane_f = lane.astype(F32)
    neg_inf = jnp.asarray(-jnp.inf, F32)

    def select_bias(qh):
        g = _dot_nt(qh, km_hi) + _dot_nt(qh, km_lo)
        g = jnp.where(lane < qi, g, neg_inf)
        sel = jnp.zeros(g.shape, jnp.bool_)
        for _ in range(MOBA_TOPK):
            mx = jnp.max(g, axis=-1, keepdims=True)
            idx = jnp.min(jnp.where(g == mx, lane_f, float(LANES)), axis=-1, keepdims=True)
            pick = (lane_f == idx) & (mx > neg_inf)
            sel = sel | pick
            g = jnp.where(pick, neg_inf, g)
        return jnp.where(sel, 0.0, MASK_BIAS).astype(_MXU_DTYPE)

    qa = jnp.concatenate([q0, q1], axis=0)
    q_aug = jnp.concatenate(
        [qa, jnp.concatenate([select_bias(q0), select_bias(q1)], axis=0)], axis=1)

    own = pl.ds(pl.multiple_of(qi * blk, blk), blk)
    s = _dot_nt(qa, k_ref[own, :])
    row = lax.broadcasted_iota(jnp.int32, s.shape, 0)
    col = lax.broadcasted_iota(jnp.int32, s.shape, 1)
    qpos = jnp.where(row >= blk, row - blk, row)
    s = jnp.where(col <= qpos, s, neg_inf)
    m0 = jnp.max(s, axis=-1, keepdims=True)
    p = jnp.exp(s - m0)
    m_ref[...] = m0
    l_ref[...] = jnp.sum(p, axis=-1, keepdims=True)
    acc_ref[...] = _dot(p.astype(_MXU_DTYPE), v_ref[own, :])

    def past_body(j, c):
        sl = pl.ds(pl.multiple_of(j * blk, blk), blk)
        onehot = jnp.where(lane == j, 1.0, 0.0).astype(_MXU_DTYPE)
        k_aug = jnp.concatenate([k_ref[sl, :], onehot], axis=1)
        sj = _dot_nt(q_aug, k_aug)
        m_old = m_ref[...]
        m_new = jnp.maximum(m_old, jnp.max(sj, axis=-1, keepdims=True))
        alpha = jnp.exp(m_old - m_new)
        pj = jnp.exp(sj - m_new)
        m_ref[...] = m_new
        l_ref[...] = alpha * l_ref[...] + jnp.sum(pj, axis=-1, keepdims=True)
        acc_ref[...] = alpha * acc_ref[...] + _dot(pj.astype(_MXU_DTYPE), v_ref[sl, :])
        return c

    lax.fori_loop(0, qi, past_body, 0)

    out = acc_ref[...] / l_ref[...]
    o_ref[...] = jnp.where(head0, out[:blk], out[blk:]).astype(o_ref.dtype)


def _moba(qkv, batch, seq):
    blk, dh = MOBA_BLOCK, ATT_HEAD_DIM
    assert seq % blk == 0 and 2 * dh == LANES
    n_blk = seq // blk
    assert n_blk <= LANES
    n_pair = GROUP_WIDTH // LANES
    return pl.pallas_call(
        functools.partial(_moba_kernel, blk=blk, dh=dh, n_blk=n_blk),
        grid=(batch, n_pair, n_blk),
        in_specs=[pl.BlockSpec((blk, LANES), lambda b, h, i: (b * n_blk + i, h)),
                  pl.BlockSpec((seq, LANES), lambda b, h, i: (b, n_pair + h)),
                  pl.BlockSpec((seq, LANES), lambda b, h, i: (b, 2 * n_pair + h))],
        out_specs=pl.BlockSpec((blk, LANES), lambda b, h, i: (b * n_blk + i, h)),
        out_shape=jax.ShapeDtypeStruct((batch * seq, GROUP_WIDTH), F32),
        scratch_shapes=[pltpu.VMEM((LANES, LANES), F32),
                        pltpu.VMEM((2 * blk, 1), F32),
                        pltpu.VMEM((2 * blk, 1), F32),
                        pltpu.VMEM((2 * blk, LANES), F32)],
        compiler_params=pltpu.CompilerParams(dimension_semantics=("arbitrary", "arbitrary", "arbitrary"),
                                             vmem_limit_bytes=VMEM_LIMIT),
    )(qkv, qkv, qkv)


def _shift_rows(x, tail, s):
    if s == 0:
        return x
    xs = pltpu.roll(x, s, axis=0)
    ts = pltpu.roll(tail, s, axis=0)
    rid = lax.broadcasted_iota(jnp.int32, tail.shape, 0)
    head = jnp.where(rid < s, ts, xs[:SUBLANES])
    return jnp.concatenate([head, xs[SUBLANES:]], axis=0)


def _causal_dwconv(x, tail, w_ref):
    kk = w_ref.shape[0]
    y = None
    for k in range(kk):
        term = w_ref[k:k + 1, :] * _shift_rows(x, tail, kk - 1 - k)
        y = term if y is None else y + term
    return y


def _rms_gate(y, gain, gate):
    r = lax.rsqrt(jnp.mean(y * y, axis=-1, keepdims=True) + RMS_EPS)
    return y * r * gain * (gate * _sigmoid(gate))


def _layer_norm(x, g, b):
    mu = jnp.mean(x, axis=-1, keepdims=True)
    xc = x - mu
    var = jnp.mean(xc * xc, axis=-1, keepdims=True)
    return xc * lax.rsqrt(var + LN_EPS) * g + b


def _mix_kernel(rest_ref, yatt_ref, x_ref, lcw_ref, lcb_ref, wg_ref, bg_ref, lam_ref, scw_ref, gain_ref,
                wout_ref, lng_ref, lnb_ref, o_ref, h_ref, tail_lru_ref, tail_sc_ref, *, ts):
    g = GROUP_WIDTH

    @pl.when(pl.program_id(1) == 0)
    def _():
        h_ref[...] = jnp.zeros_like(h_ref)
        tail_lru_ref[...] = jnp.zeros_like(tail_lru_ref)
        tail_sc_ref[...] = jnp.zeros_like(tail_sc_ref)

    x_lru = rest_ref[:, g:2 * g]
    xl = _causal_dwconv(x_lru, tail_lru_ref[...], lcw_ref) + lcb_ref[...]
    tail_lru_ref[...] = x_lru[ts - SUBLANES:, :]
    pre = _dot(xl.astype(_MXU_DTYPE), wg_ref[...]) + bg_ref[...]
    r = _sigmoid(pre[:, :g])
    ig = _sigmoid(pre[:, g:])
    neg_lam = -lam_ref[...]
    softplus = jnp.maximum(neg_lam, 0.0) + jnp.log1p(jnp.exp(-jnp.abs(neg_lam)))
    log_a = -LRU_C * r * softplus
    a = jnp.exp(log_a)
    u = jnp.sqrt(-jnp.tanh(log_a) * (a * a + 1.0)) * (ig * xl)
    rid = lax.broadcasted_iota(jnp.int32, a.shape, 0)
    d = 1
    while d < ts:
        keep = rid >= d
        u = jnp.where(keep, a * pltpu.roll(u, d, axis=0) + u, u)
        a = jnp.where(keep, a * pltpu.roll(a, d, axis=0), a)
        d *= 2
    y_lru = a * h_ref[...] + u
    h_ref[...] = y_lru[ts - 1:, :]

    z = rest_ref[:, 4 * g:5 * g] * rest_ref[:, 5 * g:6 * g]
    y_sc = rest_ref[:, 3 * g:4 * g] * _causal_dwconv(z, tail_sc_ref[...], scw_ref)
    tail_sc_ref[...] = z[ts - SUBLANES:, :]

    y = jnp.concatenate([
        _rms_gate(yatt_ref[...].astype(F32), gain_ref[0:1, :], rest_ref[:, 0:g]).astype(_MXU_DTYPE),
        _rms_gate(y_lru, gain_ref[1:2, :], rest_ref[:, 2 * g:3 * g]).astype(_MXU_DTYPE),
        _rms_gate(y_sc, gain_ref[2:3, :], rest_ref[:, 6 * g:7 * g]).astype(_MXU_DTYPE)], axis=1)
    mix = _dot(y, wout_ref[...])
    o_ref[...] = _layer_norm(DN_ALPHA * x_ref[...] + mix, lng_ref[...], lnb_ref[...])


def _mix(rest, yatt, x, lcw, lcb, wg, bg, lam, scw, gain, wout, lng, lnb, batch, seq, ts=256):
    d = x.shape[1]
    g = GROUP_WIDTH
    assert seq % ts == 0 and ts % SUBLANES == 0
    nt = seq // ts
    row = lambda b, i: (b * nt + i, 0)
    full = lambda b, i: (0, 0)
    return pl.pallas_call(
        functools.partial(_mix_kernel, ts=ts),
        grid=(batch, nt),
        in_specs=[pl.BlockSpec((ts, 7 * g), row),
                  pl.BlockSpec((ts, g), row),
                  pl.BlockSpec((ts, d), row),
                  pl.BlockSpec(lcw.shape, full),
                  pl.BlockSpec(lcb.shape, full),
                  pl.BlockSpec(wg.shape, full),
                  pl.BlockSpec(bg.shape, full),
                  pl.BlockSpec(lam.shape, full),
                  pl.BlockSpec(scw.shape, full),
                  pl.BlockSpec(gain.shape, full),
                  pl.BlockSpec(wout.shape, full),
                  pl.BlockSpec(lng.shape, full),
                  pl.BlockSpec(lnb.shape, full)],
        out_specs=pl.BlockSpec((ts, d), row),
        out_shape=jax.ShapeDtypeStruct((batch * seq, d), F32),
        scratch_shapes=[pltpu.VMEM((1, g), F32),
                        pltpu.VMEM((SUBLANES, g), F32),
                        pltpu.VMEM((SUBLANES, g), F32)],
        compiler_params=pltpu.CompilerParams(dimension_semantics=("arbitrary", "arbitrary"),
                                             vmem_limit_bytes=VMEM_LIMIT),
    )(rest, yatt, x, lcw, lcb, wg, bg, lam, scw, gain, wout, lng, lnb)


def _xattn_kernel(x_ref, kv_ref, wq_ref, wo_ref, lng_ref, lnb_ref, o_ref, *, n_heads):
    x = x_ref[...]
    d = x.shape[1]
    dh = d // n_heads
    q = (_dot(x.astype(_MXU_DTYPE), wq_ref[...]) * (dh ** -0.5)).astype(_MXU_DTYPE)
    outs = []
    for h in range(n_heads):
        s = _dot_nt(q[:, h * dh:(h + 1) * dh], kv_ref[:, h * dh:(h + 1) * dh])
        p = jnp.exp(s - jnp.max(s, axis=-1, keepdims=True))
        p = p / jnp.sum(p, axis=-1, keepdims=True)
        outs.append(_dot(p.astype(_MXU_DTYPE), kv_ref[:, d + h * dh:d + (h + 1) * dh]).astype(_MXU_DTYPE))
    xat = _dot(jnp.concatenate(outs, axis=1), wo_ref[...])
    o_ref[...] = _layer_norm(DN_ALPHA * x + xat, lng_ref[...], lnb_ref[...])


def _xattn(x, kv, wq, wo, lng, lnb, batch, seq, n_mem, ts=512):
    d = x.shape[1]
    assert seq % ts == 0
    nt = seq // ts
    dh = d // XATTN_HEADS
    assert (dh & (dh - 1)) == 0 and (dh.bit_length() - 1) % 2 == 0, "scale must be a power of two"
    row = lambda b, i: (b * nt + i, 0)
    full = lambda b, i: (0, 0)
    return pl.pallas_call(
        functools.partial(_xattn_kernel, n_heads=XATTN_HEADS),
        grid=(batch, nt),
        in_specs=[pl.BlockSpec((ts, d), row),
                  pl.BlockSpec((n_mem, 2 * d), lambda b, i: (b, 0)),
                  pl.BlockSpec(wq.shape, full),
                  pl.BlockSpec(wo.shape, full),
                  pl.BlockSpec(lng.shape, full),
                  pl.BlockSpec(lnb.shape, full)],
        out_specs=pl.BlockSpec((ts, d), row),
        out_shape=jax.ShapeDtypeStruct((batch * seq, d), F32),
        compiler_params=pltpu.CompilerParams(dimension_semantics=("arbitrary", "arbitrary"),
                                             vmem_limit_bytes=VMEM_LIMIT),
    )(x, kv, wq, wo, lng, lnb)


def _block_diag(w):
    h, d, _ = w.shape
    eye = jnp.eye(h, dtype=w.dtype)
    return (eye[:, None, :, None] * w[:, :, None, :]).reshape(h * d, h * d)


def kernel(x, mem, w_in, lru_conv_w, lru_conv_b, lru_w_a, lru_b_a, lru_w_x, lru_b_x, lru_lambda, sc_conv_w,
           group_gain, w_out, ln1_g, ln1_b, xq_w, xk_w, xv_w, xo_w, ln2_g, ln2_b):
    batch, seq, d = x.shape
    n_mem = mem.shape[1]
    g = GROUP_WIDTH
    assert ATT_HEAD_DIM ** -0.5 == 0.125
    xf = x.reshape(batch * seq, d)
    memf = mem.reshape(batch * n_mem, d)
    row = lambda v: v.reshape(1, -1)
    tm = 512 if (batch * seq) % 512 == 0 else 256
    for l in range(w_in.shape[0]):
        w_in_l = w_in[l].astype(_MXU_DTYPE)
        qkv = _matmul(xf, w_in_l[:, :3 * g], _MXU_DTYPE, tm)
        rest = _matmul(xf, w_in_l[:, 3 * g:], F32, tm)
        yatt = _moba(qkv, batch, seq)
        wg = jnp.concatenate([_block_diag(lru_w_a[l]), _block_diag(lru_w_x[l])], axis=1).astype(_MXU_DTYPE)
        bg = jnp.concatenate([lru_b_a[l], lru_b_x[l]]).reshape(1, -1)
        x1 = _mix(rest, yatt, xf, lru_conv_w[l], row(lru_conv_b[l]), wg, bg, row(lru_lambda[l]),
                  sc_conv_w[l], group_gain[l], w_out[l].astype(_MXU_DTYPE), row(ln1_g[l]), row(ln1_b[l]),
                  batch, seq)
        wkv = jnp.concatenate([xk_w[l], xv_w[l]], axis=1).astype(_MXU_DTYPE)
        kv = _matmul(memf, wkv, _MXU_DTYPE, n_mem)
        xf = _xattn(x1, kv, xq_w[l].astype(_MXU_DTYPE), xo_w[l].astype(_MXU_DTYPE), row(ln2_g[l]),
                    row(ln2_b[l]), batch, seq, n_mem)
    return xf.reshape(batch, seq, d)
```

```python
import functools

import jax
import jax.numpy as jnp
from jax import lax
from jax.experimental import pallas as pl
from jax.experimental.pallas import tpu as pltpu

F32 = jnp.float32
_MXU_DTYPE = jnp.bfloat16

GROUP_WIDTH = 512
ATT_HEAD_DIM = 64
MOBA_BLOCK = 256
MOBA_TOPK = 3
LRU_HEADS = 8
LRU_C = 8.0
XATTN_HEADS = 4
DEPTH = 2
DN_ALPHA = (2.0 * DEPTH) ** 0.25
LN_EPS = 1e-5
RMS_EPS = 1e-6

LANES = 128
SUBLANES = 8
PACKED_ROWS = 16
VMEM_LIMIT = 48 * 1024 * 1024
MASK_BIAS = -1e30
MOBA_UNROLL = 4


def _dot(a, b):
    return jnp.dot(a, b, preferred_element_type=F32)


def _dot_nt(a, b):
    return lax.dot_general(a, b, (((1,), (1,)), ((), ())), preferred_element_type=F32)


def _sigmoid(x):
    return 1.0 / (1.0 + jnp.exp(-x))


def _matmul_kernel(x_ref, w_ref, o_ref, *, n_chunk):
    xb = x_ref[...].astype(_MXU_DTYPE)
    n = w_ref.shape[1]
    for c in range(0, n, n_chunk):
        o_ref[:, c:c + n_chunk] = _dot(xb, w_ref[:, c:c + n_chunk]).astype(o_ref.dtype)


def _matmul(x, w, out_dtype, tm, n_chunk=512):
    m, k = x.shape
    n = w.shape[1]
    assert m % tm == 0 and n % n_chunk == 0
    return pl.pallas_call(
        functools.partial(_matmul_kernel, n_chunk=n_chunk),
        grid=(m // tm,),
        in_specs=[pl.BlockSpec((tm, k), lambda i: (i, 0)),
                  pl.BlockSpec((k, n), lambda i: (0, 0))],
        out_specs=pl.BlockSpec((tm, n), lambda i: (i, 0)),
        out_shape=jax.ShapeDtypeStruct((m, n), out_dtype),
        compiler_params=pltpu.CompilerParams(dimension_semantics=("arbitrary",),
                                             vmem_limit_bytes=VMEM_LIMIT),
    )(x, w)


def _qkv_kernel(x_ref, wqt_ref, wk_ref, wvt_ref, qt_ref, k_ref, vt_ref, *, blk):
    xb = x_ref[...].astype(_MXU_DTYPE)
    k_ref[...] = _dot(xb, wk_ref[...]).astype(k_ref.dtype)
    for c in range(x_ref.shape[0] // blk):
        xc = xb[c * blk:(c + 1) * blk, :]
        qt_ref[c] = _dot_nt(wqt_ref[...], xc).astype(qt_ref.dtype)
        vt_ref[c] = _dot_nt(wvt_ref[...], xc).astype(vt_ref.dtype)


def _qkv_proj(x, wqt, wk, wvt, tm, blk):
    m, k = x.shape
    g = wk.shape[1]
    assert m % tm == 0 and tm % blk == 0
    full = lambda i: (0, 0)
    slab = jax.ShapeDtypeStruct((m // blk, g, blk), _MXU_DTYPE)
    return pl.pallas_call(
        functools.partial(_qkv_kernel, blk=blk),
        grid=(m // tm,),
        in_specs=[pl.BlockSpec((tm, k), lambda i: (i, 0)),
                  pl.BlockSpec(wqt.shape, full),
                  pl.BlockSpec(wk.shape, full),
                  pl.BlockSpec(wvt.shape, full)],
        out_specs=[pl.BlockSpec((tm // blk, g, blk), lambda i: (i, 0, 0)),
                   pl.BlockSpec((tm, g), lambda i: (i, 0)),
                   pl.BlockSpec((tm // blk, g, blk), lambda i: (i, 0, 0))],
        out_shape=[slab, jax.ShapeDtypeStruct((m, g), _MXU_DTYPE), slab],
        compiler_params=pltpu.CompilerParams(dimension_semantics=("arbitrary",),
                                             vmem_limit_bytes=VMEM_LIMIT),
    )(x, wqt, wk, wvt)


def _moba_kernel(qt_ref, k_ref, vt_ref, o_ref, kmean_ref, oh_ref, vat_ref, qat_ref, m_ref, acc_ref, s_ref,
                 *, blk, dh, n_blk, unroll):
    qi = pl.program_id(2)
    n_ones = PACKED_ROWS

    @pl.when(qi == 0)
    def _():
        kmean_ref[...] = jnp.zeros_like(kmean_ref)
        lane = lax.broadcasted_iota(jnp.int32, (blk, LANES), 1)
        ones = jnp.ones((n_ones, blk), vat_ref.dtype)

        def prep_body(j, c):
            sl = pl.ds(pl.multiple_of(j * blk, blk), blk)
            kb = k_ref[sl, :].astype(F32)
            kmean_ref[pl.ds(j, 1), :] = jnp.sum(kb, axis=0, keepdims=True) * (1.0 / blk)
            oh_ref[sl, :] = jnp.where(lane == j, 1.0, 0.0).astype(oh_ref.dtype)
            for h in range(2):
                vat_ref[h, j] = jnp.concatenate([vt_ref[j, h * dh:(h + 1) * dh, :], ones], axis=0)
            return c

        lax.fori_loop(0, n_blk, prep_body, 0)

    scale = dh ** -0.5
    qt = qt_ref[0] * jnp.asarray(scale, qt_ref.dtype)
    frow = lax.broadcasted_iota(jnp.int32, qt.shape, 0)
    zero = jnp.zeros_like(qt)

    km = kmean_ref[...]
    km_hi = km.astype(_MXU_DTYPE)
    km_lo = (km - km_hi.astype(F32)).astype(_MXU_DTYPE)
    brow = lax.broadcasted_iota(jnp.int32, (n_blk, blk), 0).astype(F32)
    neg_inf = jnp.asarray(-jnp.inf, F32)
    own = pl.ds(pl.multiple_of(qi * blk, blk), blk)
    krow = lax.broadcasted_iota(jnp.int32, (blk, blk), 0)
    qcol = lax.broadcasted_iota(jnp.int32, (blk, blk), 1)

    for h in range(2):
        qth = jnp.where((frow >= h * dh) & (frow < (h + 1) * dh), qt, zero)
        g = (_dot(km_hi, qth) + _dot(km_lo, qth))[:n_blk]
        g = jnp.where(brow < qi.astype(F32), g, neg_inf)
        sel = jnp.zeros(g.shape, jnp.bool_)
        for _ in range(MOBA_TOPK):
            mx = jnp.max(g, axis=0, keepdims=True)
            idx = jnp.min(jnp.where(g == mx, brow, float(n_blk)), axis=0, keepdims=True)
            pick = (brow == idx) & (mx > neg_inf)
            sel = sel | pick
            g = jnp.where(pick, neg_inf, g)
        bias = jnp.where(sel, 0.0, MASK_BIAS).astype(_MXU_DTYPE)
        qat_ref[h] = jnp.concatenate(
            [qth, bias, jnp.zeros((LANES - n_blk, blk), _MXU_DTYPE)], axis=0)

        s = _dot(k_ref[own, :], qth)
        s = jnp.where(krow <= qcol, s, neg_inf)
        m0 = jnp.max(s, axis=0, keepdims=True)
        m_ref[h] = m0
        acc_ref[h] = _dot(vat_ref[h, qi], jnp.exp(s - m0).astype(_MXU_DTYPE))

    def scores(gi, u):
        j = jnp.minimum(gi * unroll + u // 2, n_blk - 1)
        sl = pl.ds(pl.multiple_of(j * blk, blk), blk)
        k_aug = jnp.concatenate([k_ref[sl, :], oh_ref[sl, :]], axis=1)
        return _dot(k_aug, qat_ref[u % 2])

    @pl.when(qi > 0)
    def _():
        for u in range(2 * unroll):
            s_ref[u] = scores(0, u)

    def group_body(gi, c):
        for u in range(2 * unroll):
            h = u % 2
            m_old = m_ref[h]
            m_new = jnp.maximum(m_old, jnp.max(s_ref[u], axis=0, keepdims=True))
            p = jnp.exp(s_ref[u] - m_new).astype(_MXU_DTYPE)
            s_ref[u] = scores(gi + 1, u)
            m_ref[h] = m_new
            acc_ref[h] = jnp.exp(m_old - m_new) * acc_ref[h] + _dot(vat_ref[h, gi * unroll + u // 2], p)
        return c

    lax.fori_loop(0, (qi + unroll - 1) // unroll, group_body, 0)

    outs = []
    for h in range(2):
        acc = acc_ref[h]
        outs.append(acc[:dh] / acc[dh:dh + 1])
    o_ref[...] = jnp.concatenate(outs, axis=0).T.astype(o_ref.dtype)


def _moba(qt, k, vt, batch, seq):
    blk, dh, unroll = MOBA_BLOCK, ATT_HEAD_DIM, MOBA_UNROLL
    assert seq % blk == 0 and 2 * dh == LANES
    n_blk = seq // blk
    assert n_blk <= LANES and n_blk % unroll == 0
    n_pair = GROUP_WIDTH // LANES
    return pl.pallas_call(
        functools.partial(_moba_kernel, blk=blk, dh=dh, n_blk=n_blk, unroll=unroll),
        grid=(batch, n_pair, n_blk),
        in_specs=[pl.BlockSpec((1, LANES, blk), lambda b, h, i: (b * n_blk + i, h, 0)),
                  pl.BlockSpec((seq, LANES), lambda b, h, i: (b, h)),
                  pl.BlockSpec((n_blk, LANES, blk), lambda b, h, i: (b, h, 0))],
        out_specs=pl.BlockSpec((blk, LANES), lambda b, h, i: (b * n_blk + i, h)),
        out_shape=jax.ShapeDtypeStruct((batch * seq, GROUP_WIDTH), F32),
        scratch_shapes=[pltpu.VMEM((LANES, LANES), F32),
                        pltpu.VMEM((seq, LANES), _MXU_DTYPE),
                        pltpu.VMEM((2, n_blk, dh + PACKED_ROWS, blk), _MXU_DTYPE),
                        pltpu.VMEM((2, 2 * LANES, blk), _MXU_DTYPE),
                        pltpu.VMEM((2, 1, blk), F32),
                        pltpu.VMEM((2, dh + PACKED_ROWS, blk), F32),
                        pltpu.VMEM((2 * unroll, blk, blk), F32)],
        compiler_params=pltpu.CompilerParams(dimension_semantics=("arbitrary", "arbitrary", "arbitrary"),
                                             vmem_limit_bytes=VMEM_LIMIT),
    )(qt, k, vt)


def _shift_rows(x, tail, s):
    if s == 0:
        return x
    xs = pltpu.roll(x, s, axis=0)
    ts = pltpu.roll(tail, s, axis=0)
    rid = lax.broadcasted_iota(jnp.int32, tail.shape, 0)
    head = jnp.where(rid < s, ts, xs[:SUBLANES])
    return jnp.concatenate([head, xs[SUBLANES:]], axis=0)


def _causal_dwconv(x, tail, w_ref):
    kk = w_ref.shape[0]
    y = None
    for k in range(kk):
        term = w_ref[k:k + 1, :] * _shift_rows(x, tail, kk - 1 - k)
        y = term if y is None else y + term
    return y


def _rms_gate(y, gain, gate):
    r = lax.rsqrt(jnp.mean(y * y, axis=-1, keepdims=True) + RMS_EPS)
    return y * r * gain * (gate * _sigmoid(gate))


def _layer_norm(x, g, b):
    mu = jnp.mean(x, axis=-1, keepdims=True)
    xc = x - mu
    var = jnp.mean(xc * xc, axis=-1, keepdims=True)
    return xc * lax.rsqrt(var + LN_EPS) * g + b


def _mix_kernel(rest_ref, yatt_ref, x_ref, lcw_ref, lcb_ref, wg_ref, bg_ref, lam_ref, scw_ref, gain_ref,
                wout_ref, lng_ref, lnb_ref, o_ref, h_ref, tail_lru_ref, tail_sc_ref, *, ts):
    g = GROUP_WIDTH

    @pl.when(pl.program_id(1) == 0)
    def _():
        h_ref[...] = jnp.zeros_like(h_ref)
        tail_lru_ref[...] = jnp.zeros_like(tail_lru_ref)
        tail_sc_ref[...] = jnp.zeros_like(tail_sc_ref)

    x_lru = rest_ref[:, g:2 * g]
    xl = _causal_dwconv(x_lru, tail_lru_ref[...], lcw_ref) + lcb_ref[...]
    tail_lru_ref[...] = x_lru[ts - SUBLANES:, :]
    pre = _dot(xl.astype(_MXU_DTYPE), wg_ref[...]) + bg_ref[...]
    r = _sigmoid(pre[:, :g])
    ig = _sigmoid(pre[:, g:])
    neg_lam = -lam_ref[...]
    softplus = jnp.maximum(neg_lam, 0.0) + jnp.log1p(jnp.exp(-jnp.abs(neg_lam)))
    log_a = -LRU_C * r * softplus
    a = jnp.exp(log_a)
    u = jnp.sqrt(-jnp.tanh(log_a) * (a * a + 1.0)) * (ig * xl)
    rid = lax.broadcasted_iota(jnp.int32, a.shape, 0)
    d = 1
    while d < ts:
        keep = rid >= d
        u = jnp.where(keep, a * pltpu.roll(u, d, axis=0) + u, u)
        a = jnp.where(keep, a * pltpu.roll(a, d, axis=0), a)
        d *= 2
    y_lru = a * h_ref[...] + u
    h_ref[...] = y_lru[ts - 1:, :]

    z = rest_ref[:, 4 * g:5 * g] * rest_ref[:, 5 * g:6 * g]
    y_sc = rest_ref[:, 3 * g:4 * g] * _causal_dwconv(z, tail_sc_ref[...], scw_ref)
    tail_sc_ref[...] = z[ts - SUBLANES:, :]

    y = jnp.concatenate([
        _rms_gate(yatt_ref[...].astype(F32), gain_ref[0:1, :], rest_ref[:, 0:g]).astype(_MXU_DTYPE),
        _rms_gate(y_lru, gain_ref[1:2, :], rest_ref[:, 2 * g:3 * g]).astype(_MXU_DTYPE),
        _rms_gate(y_sc, gain_ref[2:3, :], rest_ref[:, 6 * g:7 * g]).astype(_MXU_DTYPE)], axis=1)
    mix = _dot(y, wout_ref[...])
    o_ref[...] = _layer_norm(DN_ALPHA * x_ref[...] + mix, lng_ref[...], lnb_ref[...])


def _mix(rest, yatt, x, lcw, lcb, wg, bg, lam, scw, gain, wout, lng, lnb, batch, seq, ts=256):
    d = x.shape[1]
    g = GROUP_WIDTH
    assert seq % ts == 0 and ts % SUBLANES == 0
    nt = seq // ts
    row = lambda b, i: (b * nt + i, 0)
    full = lambda b, i: (0, 0)
    return pl.pallas_call(
        functools.partial(_mix_kernel, ts=ts),
        grid=(batch, nt),
        in_specs=[pl.BlockSpec((ts, 7 * g), row),
                  pl.BlockSpec((ts, g), row),
                  pl.BlockSpec((ts, d), row),
                  pl.BlockSpec(lcw.shape, full),
                  pl.BlockSpec(lcb.shape, full),
                  pl.BlockSpec(wg.shape, full),
                  pl.BlockSpec(bg.shape, full),
                  pl.BlockSpec(lam.shape, full),
                  pl.BlockSpec(scw.shape, full),
                  pl.BlockSpec(gain.shape, full),
                  pl.BlockSpec(wout.shape, full),
                  pl.BlockSpec(lng.shape, full),
                  pl.BlockSpec(lnb.shape, full)],
        out_specs=pl.BlockSpec((ts, d), row),
        out_shape=jax.ShapeDtypeStruct((batch * seq, d), F32),
        scratch_shapes=[pltpu.VMEM((1, g), F32),
                        pltpu.VMEM((SUBLANES, g), F32),
                        pltpu.VMEM((SUBLANES, g), F32)],
        compiler_params=pltpu.CompilerParams(dimension_semantics=("arbitrary", "arbitrary"),
                                             vmem_limit_bytes=VMEM_LIMIT),
    )(rest, yatt, x, lcw, lcb, wg, bg, lam, scw, gain, wout, lng, lnb)


def _xattn_kernel(x_ref, kv_ref, wq_ref, wo_ref, lng_ref, lnb_ref, o_ref, *, n_heads):
    x = x_ref[...]
    d = x.shape[1]
    dh = d // n_heads
    q = (_dot(x.astype(_MXU_DTYPE), wq_ref[...]) * (dh ** -0.5)).astype(_MXU_DTYPE)
    outs = []
    for h in range(n_heads):
        s = _dot_nt(q[:, h * dh:(h + 1) * dh], kv_ref[:, h * dh:(h + 1) * dh])
        p = jnp.exp(s - jnp.max(s, axis=-1, keepdims=True))
        p = p / jnp.sum(p, axis=-1, keepdims=True)
        outs.append(_dot(p.astype(_MXU_DTYPE), kv_ref[:, d + h * dh:d + (h + 1) * dh]).astype(_MXU_DTYPE))
    xat = _dot(jnp.concatenate(outs, axis=1), wo_ref[...])
    o_ref[...] = _layer_norm(DN_ALPHA * x + xat, lng_ref[...], lnb_ref[...])


def _xattn(x, kv, wq, wo, lng, lnb, batch, seq, n_mem, ts=512):
    d = x.shape[1]
    assert seq % ts == 0
    nt = seq // ts
    dh = d // XATTN_HEADS
    assert (dh & (dh - 1)) == 0 and (dh.bit_length() - 1) % 2 == 0, "scale must be a power of two"
    row = lambda b, i: (b * nt + i, 0)
    full = lambda b, i: (0, 0)
    return pl.pallas_call(
        functools.partial(_xattn_kernel, n_heads=XATTN_HEADS),
        grid=(batch, nt),
        in_specs=[pl.BlockSpec((ts, d), row),
                  pl.BlockSpec((n_mem, 2 * d), lambda b, i: (b, 0)),
                  pl.BlockSpec(wq.shape, full),
                  pl.BlockSpec(wo.shape, full),
                  pl.BlockSpec(lng.shape, full),
                  pl.BlockSpec(lnb.shape, full)],
        out_specs=pl.BlockSpec((ts, d), row),
        out_shape=jax.ShapeDtypeStruct((batch * seq, d), F32),
        compiler_params=pltpu.CompilerParams(dimension_semantics=("arbitrary", "arbitrary"),
                                             vmem_limit_bytes=VMEM_LIMIT),
    )(x, kv, wq, wo, lng, lnb)


def _block_diag(w):
    h, d, _ = w.shape
    eye = jnp.eye(h, dtype=w.dtype)
    return (eye[:, None, :, None] * w[:, :, None, :]).reshape(h * d, h * d)


def kernel(x, mem, w_in, lru_conv_w, lru_conv_b, lru_w_a, lru_b_a, lru_w_x, lru_b_x, lru_lambda, sc_conv_w,
           group_gain, w_out, ln1_g, ln1_b, xq_w, xk_w, xv_w, xo_w, ln2_g, ln2_b):
    batch, seq, d = x.shape
    n_mem = mem.shape[1]
    g = GROUP_WIDTH
    assert ATT_HEAD_DIM ** -0.5 == 0.125
    xf = x.reshape(batch * seq, d)
    memf = mem.reshape(batch * n_mem, d)
    row = lambda v: v.reshape(1, -1)
    tm = 512
    for l in range(w_in.shape[0]):
        w_in_l = w_in[l].astype(_MXU_DTYPE)
        qt, k, vt = _qkv_proj(xf, w_in_l[:, :g].T, w_in_l[:, g:2 * g], w_in_l[:, 2 * g:3 * g].T, tm, MOBA_BLOCK)
        rest = _matmul(xf, w_in_l[:, 3 * g:], F32, tm)
        yatt = _moba(qt, k, vt, batch, seq)
        wg = jnp.concatenate([_block_diag(lru_w_a[l]), _block_diag(lru_w_x[l])], axis=1).astype(_MXU_DTYPE)
        bg = jnp.concatenate([lru_b_a[l], lru_b_x[l]]).reshape(1, -1)
        x1 = _mix(rest, yatt, xf, lru_conv_w[l], row(lru_conv_b[l]), wg, bg, row(lru_lambda[l]),
                  sc_conv_w[l], group_gain[l], w_out[l].astype(_MXU_DTYPE), row(ln1_g[l]), row(ln1_b[l]),
                  batch, seq)
        wkv = jnp.concatenate([xk_w[l], xv_w[l]], axis=1).astype(_MXU_DTYPE)
        kv = _matmul(memf, wkv, _MXU_DTYPE, n_mem)
        xf = _xattn(x1, kv, xq_w[l].astype(_MXU_DTYPE), xo_w[l].astype(_MXU_DTYPE), row(ln2_g[l]),
                    row(ln2_b[l]), batch, seq, n_mem)
    return xf.reshape(batch, seq, d)
```

```python
import functools
import math

import jax
import jax.numpy as jnp
from jax import lax
from jax.experimental import pallas as pl
from jax.experimental.pallas import tpu as pltpu

F32 = jnp.float32
_MXU_DTYPE = jnp.bfloat16

GROUP_WIDTH = 512
ATT_HEAD_DIM = 64
MOBA_BLOCK = 256
MOBA_TOPK = 3
LRU_HEADS = 8
LRU_C = 8.0
XATTN_HEADS = 4
DEPTH = 2
DN_ALPHA = (2.0 * DEPTH) ** 0.25
LN_EPS = 1e-5
RMS_EPS = 1e-6

LANES = 128
SUBLANES = 8
PACKED_ROWS = 16
VMEM_LIMIT = 48 * 1024 * 1024
MASK_BIAS = -1e30
MOBA_UNROLL = 4


def _dot(a, b):
    return jnp.dot(a, b, preferred_element_type=F32)


def _dot_nt(a, b):
    return lax.dot_general(a, b, (((1,), (1,)), ((), ())), preferred_element_type=F32)


def _sigmoid(x):
    return 1.0 / (1.0 + jnp.exp(-x))


def _matmul_kernel(x_ref, w_ref, o_ref, *, n_chunk):
    xb = x_ref[...].astype(_MXU_DTYPE)
    n = w_ref.shape[1]
    for c in range(0, n, n_chunk):
        o_ref[:, c:c + n_chunk] = _dot(xb, w_ref[:, c:c + n_chunk]).astype(o_ref.dtype)


def _matmul(x, w, out_dtype, tm, n_chunk=512):
    m, k = x.shape
    n = w.shape[1]
    assert m % tm == 0 and n % n_chunk == 0
    return pl.pallas_call(
        functools.partial(_matmul_kernel, n_chunk=n_chunk),
        grid=(m // tm,),
        in_specs=[pl.BlockSpec((tm, k), lambda i: (i, 0)),
                  pl.BlockSpec((k, n), lambda i: (0, 0))],
        out_specs=pl.BlockSpec((tm, n), lambda i: (i, 0)),
        out_shape=jax.ShapeDtypeStruct((m, n), out_dtype),
        compiler_params=pltpu.CompilerParams(dimension_semantics=("arbitrary",),
                                             vmem_limit_bytes=VMEM_LIMIT),
    )(x, w)


def _qkv_kernel(x_ref, wqt_ref, wk_ref, wvt_ref, qt_ref, k_ref, vt_ref, *, blk, q_scale):
    xb = x_ref[...].astype(_MXU_DTYPE)
    k_ref[...] = _dot(xb, wk_ref[...]).astype(k_ref.dtype)
    for c in range(x_ref.shape[0] // blk):
        xc = xb[c * blk:(c + 1) * blk, :]
        qt_ref[c] = (_dot_nt(wqt_ref[...], xc) * q_scale).astype(qt_ref.dtype)
        vt_ref[c] = _dot_nt(wvt_ref[...], xc).astype(vt_ref.dtype)


def _qkv_proj(x, wqt, wk, wvt, tm, blk, q_scale):
    m, k = x.shape
    g = wk.shape[1]
    assert m % tm == 0 and tm % blk == 0
    full = lambda i: (0, 0)
    slab = jax.ShapeDtypeStruct((m // blk, g, blk), _MXU_DTYPE)
    return pl.pallas_call(
        functools.partial(_qkv_kernel, blk=blk, q_scale=q_scale),
        grid=(m // tm,),
        in_specs=[pl.BlockSpec((tm, k), lambda i: (i, 0)),
                  pl.BlockSpec(wqt.shape, full),
                  pl.BlockSpec(wk.shape, full),
                  pl.BlockSpec(wvt.shape, full)],
        out_specs=[pl.BlockSpec((tm // blk, g, blk), lambda i: (i, 0, 0)),
                   pl.BlockSpec((tm, g), lambda i: (i, 0)),
                   pl.BlockSpec((tm // blk, g, blk), lambda i: (i, 0, 0))],
        out_shape=[slab, jax.ShapeDtypeStruct((m, g), _MXU_DTYPE), slab],
        compiler_params=pltpu.CompilerParams(dimension_semantics=("arbitrary",),
                                             vmem_limit_bytes=VMEM_LIMIT),
    )(x, wqt, wk, wvt)


def _moba_kernel(qt_ref, k_ref, vt_ref, o_ref, kmean_ref, oh_ref, vat_ref, qat_ref, m_ref, acc_ref, s_ref,
                 *, blk, dh, n_blk, unroll):
    qi = pl.program_id(2)
    n_ones = PACKED_ROWS

    @pl.when(qi == 0)
    def _():
        kmean_ref[...] = jnp.zeros_like(kmean_ref)
        lane = lax.broadcasted_iota(jnp.int32, (blk, LANES), 1)
        ones = jnp.ones((n_ones, blk), vat_ref.dtype)

        def prep_body(j, c):
            sl = pl.ds(pl.multiple_of(j * blk, blk), blk)
            kb = k_ref[sl, :].astype(F32)
            kmean_ref[pl.ds(j, 1), :] = jnp.sum(kb, axis=0, keepdims=True) * (1.0 / blk)
            oh_ref[sl, :] = jnp.where(lane == j, 1.0, 0.0).astype(oh_ref.dtype)
            for h in range(2):
                vat_ref[h, j] = jnp.concatenate([vt_ref[j, h * dh:(h + 1) * dh, :], ones], axis=0)
            return c

        lax.fori_loop(0, n_blk, prep_body, 0)

    qt = qt_ref[0]
    frow = lax.broadcasted_iota(jnp.int32, qt.shape, 0)
    zero = jnp.zeros_like(qt)

    km = kmean_ref[...]
    km_hi = km.astype(_MXU_DTYPE)
    km_lo = (km - km_hi.astype(F32)).astype(_MXU_DTYPE)
    brow = lax.broadcasted_iota(jnp.int32, (n_blk, blk), 0).astype(F32)
    neg_inf = jnp.asarray(-jnp.inf, F32)
    own = pl.ds(pl.multiple_of(qi * blk, blk), blk)
    krow = lax.broadcasted_iota(jnp.int32, (blk, blk), 0)
    qcol = lax.broadcasted_iota(jnp.int32, (blk, blk), 1)

    for h in range(2):
        qth = jnp.where((frow >= h * dh) & (frow < (h + 1) * dh), qt, zero)
        g = (_dot(km_hi, qth) + _dot(km_lo, qth))[:n_blk]
        g = jnp.where(brow < qi.astype(F32), g, neg_inf)
        sel = jnp.zeros(g.shape, jnp.bool_)
        for _ in range(MOBA_TOPK):
            mx = jnp.max(g, axis=0, keepdims=True)
            idx = jnp.min(jnp.where(g == mx, brow, float(n_blk)), axis=0, keepdims=True)
            pick = (brow == idx) & (mx > neg_inf)
            sel = sel | pick
            g = jnp.where(pick, neg_inf, g)
        bias = jnp.where(sel, 0.0, MASK_BIAS).astype(_MXU_DTYPE)
        qat_ref[h] = jnp.concatenate(
            [qth, bias, jnp.zeros((LANES - n_blk, blk), _MXU_DTYPE)], axis=0)

        s = _dot(k_ref[own, :], qth)
        s = jnp.where(krow <= qcol, s, neg_inf)
        m0 = jnp.max(s, axis=0, keepdims=True)
        m_ref[h] = m0
        acc_ref[h] = _dot(vat_ref[h, qi], jnp.exp2(s - m0).astype(_MXU_DTYPE))

    def scores(gi, u):
        j = gi * unroll + u // 2
        sl = pl.ds(pl.multiple_of(j * blk, blk), blk)
        k_aug = jnp.concatenate([k_ref[sl, :], oh_ref[sl, :]], axis=1)
        return _dot(k_aug, qat_ref[u % 2])

    for u in range(2 * unroll):
        s_ref[u] = scores(0, u)

    def group_step(gi, look_ahead):
        for u in range(2 * unroll):
            h = u % 2
            m_old = m_ref[h]
            m_new = jnp.maximum(m_old, jnp.max(s_ref[u], axis=0, keepdims=True))
            p = jnp.exp2(s_ref[u] - m_new).astype(_MXU_DTYPE)
            if look_ahead:
                s_ref[u] = scores(gi + 1, u)
            m_ref[h] = m_new
            acc_ref[h] = jnp.exp2(m_old - m_new) * acc_ref[h] + _dot(vat_ref[h, gi * unroll + u // 2], p)

    n_groups = (qi + unroll - 1) // unroll

    def group_body(gi, c):
        group_step(gi, True)
        return c

    lax.fori_loop(0, n_groups - 1, group_body, 0)

    @pl.when(n_groups > 0)
    def _():
        group_step(n_groups - 1, False)

    outs = []
    for h in range(2):
        acc = acc_ref[h]
        outs.append(acc[:dh] / acc[dh:dh + 1])
    o_ref[...] = jnp.concatenate(outs, axis=0).T.astype(o_ref.dtype)


def _moba(qt, k, vt, batch, seq):
    blk, dh, unroll = MOBA_BLOCK, ATT_HEAD_DIM, MOBA_UNROLL
    assert seq % blk == 0 and 2 * dh == LANES
    n_blk = seq // blk
    assert n_blk <= LANES and n_blk % unroll == 0
    n_pair = GROUP_WIDTH // LANES
    return pl.pallas_call(
        functools.partial(_moba_kernel, blk=blk, dh=dh, n_blk=n_blk, unroll=unroll),
        grid=(batch, n_pair, n_blk),
        in_specs=[pl.BlockSpec((1, LANES, blk), lambda b, h, i: (b * n_blk + i, h, 0)),
                  pl.BlockSpec((seq, LANES), lambda b, h, i: (b, h)),
                  pl.BlockSpec((n_blk, LANES, blk), lambda b, h, i: (b, h, 0))],
        out_specs=pl.BlockSpec((blk, LANES), lambda b, h, i: (b * n_blk + i, h)),
        out_shape=jax.ShapeDtypeStruct((batch * seq, GROUP_WIDTH), F32),
        scratch_shapes=[pltpu.VMEM((LANES, LANES), F32),
                        pltpu.VMEM((seq, LANES), _MXU_DTYPE),
                        pltpu.VMEM((2, n_blk, dh + PACKED_ROWS, blk), _MXU_DTYPE),
                        pltpu.VMEM((2, 2 * LANES, blk), _MXU_DTYPE),
                        pltpu.VMEM((2, 1, blk), F32),
                        pltpu.VMEM((2, dh + PACKED_ROWS, blk), F32),
                        pltpu.VMEM((2 * unroll, blk, blk), F32)],
        compiler_params=pltpu.CompilerParams(dimension_semantics=("arbitrary", "arbitrary", "arbitrary"),
                                             vmem_limit_bytes=VMEM_LIMIT),
    )(qt, k, vt)


def _shift_rows(x, tail, s):
    if s == 0:
        return x
    xs = pltpu.roll(x, s, axis=0)
    ts = pltpu.roll(tail, s, axis=0)
    rid = lax.broadcasted_iota(jnp.int32, tail.shape, 0)
    head = jnp.where(rid < s, ts, xs[:SUBLANES])
    return jnp.concatenate([head, xs[SUBLANES:]], axis=0)


def _causal_dwconv(x, tail, w_ref):
    kk = w_ref.shape[0]
    y = None
    for k in range(kk):
        term = w_ref[k:k + 1, :] * _shift_rows(x, tail, kk - 1 - k)
        y = term if y is None else y + term
    return y


def _rms_gate(y, gain, gate):
    r = lax.rsqrt(jnp.mean(y * y, axis=-1, keepdims=True) + RMS_EPS)
    return y * r * gain * (gate * _sigmoid(gate))


def _layer_norm(x, g, b):
    mu = jnp.mean(x, axis=-1, keepdims=True)
    xc = x - mu
    var = jnp.mean(xc * xc, axis=-1, keepdims=True)
    return xc * lax.rsqrt(var + LN_EPS) * g + b


def _mix_kernel(rest_ref, yatt_ref, x_ref, lcw_ref, lcb_ref, wg_ref, bg_ref, lam_ref, scw_ref, gain_ref,
                wout_ref, lng_ref, lnb_ref, o_ref, h_ref, tail_lru_ref, tail_sc_ref, *, ts):
    g = GROUP_WIDTH

    @pl.when(pl.program_id(1) == 0)
    def _():
        h_ref[...] = jnp.zeros_like(h_ref)
        tail_lru_ref[...] = jnp.zeros_like(tail_lru_ref)
        tail_sc_ref[...] = jnp.zeros_like(tail_sc_ref)

    x_lru = rest_ref[:, g:2 * g]
    xl = _causal_dwconv(x_lru, tail_lru_ref[...], lcw_ref) + lcb_ref[...]
    tail_lru_ref[...] = x_lru[ts - SUBLANES:, :]
    pre = _dot(xl.astype(_MXU_DTYPE), wg_ref[...]) + bg_ref[...]
    r = _sigmoid(pre[:, :g])
    ig = _sigmoid(pre[:, g:])
    neg_lam = -lam_ref[...]
    softplus = jnp.maximum(neg_lam, 0.0) + jnp.log1p(jnp.exp(-jnp.abs(neg_lam)))
    log_a = -LRU_C * r * softplus
    a = jnp.exp(log_a)
    u = jnp.sqrt(-jnp.tanh(log_a) * (a * a + 1.0)) * (ig * xl)
    rid = lax.broadcasted_iota(jnp.int32, a.shape, 0)
    d = 1
    while d < ts:
        if d < SUBLANES:
            keep = rid >= d
            u = jnp.where(keep, a * pltpu.roll(u, d, axis=0) + u, u)
            a = jnp.where(keep, a * pltpu.roll(a, d, axis=0), a)
        else:
            u = jnp.concatenate([u[:d], a[d:] * u[:ts - d] + u[d:]], axis=0)
            a = jnp.concatenate([a[:d], a[d:] * a[:ts - d]], axis=0)
        d *= 2
    y_lru = a * h_ref[...] + u
    h_ref[...] = y_lru[ts - 1:, :]

    z = rest_ref[:, 4 * g:5 * g] * rest_ref[:, 5 * g:6 * g]
    y_sc = rest_ref[:, 3 * g:4 * g] * _causal_dwconv(z, tail_sc_ref[...], scw_ref)
    tail_sc_ref[...] = z[ts - SUBLANES:, :]

    y = jnp.concatenate([
        _rms_gate(yatt_ref[...].astype(F32), gain_ref[0:1, :], rest_ref[:, 0:g]).astype(_MXU_DTYPE),
        _rms_gate(y_lru, gain_ref[1:2, :], rest_ref[:, 2 * g:3 * g]).astype(_MXU_DTYPE),
        _rms_gate(y_sc, gain_ref[2:3, :], rest_ref[:, 6 * g:7 * g]).astype(_MXU_DTYPE)], axis=1)
    mix = _dot(y, wout_ref[...])
    o_ref[...] = _layer_norm(DN_ALPHA * x_ref[...] + mix, lng_ref[...], lnb_ref[...])


def _mix(rest, yatt, x, lcw, lcb, wg, bg, lam, scw, gain, wout, lng, lnb, batch, seq, ts=256):
    d = x.shape[1]
    g = GROUP_WIDTH
    assert seq % ts == 0 and ts % SUBLANES == 0
    nt = seq // ts
    row = lambda b, i: (b * nt + i, 0)
    full = lambda b, i: (0, 0)
    return pl.pallas_call(
        functools.partial(_mix_kernel, ts=ts),
        grid=(batch, nt),
        in_specs=[pl.BlockSpec((ts, 7 * g), row),
                  pl.BlockSpec((ts, g), row),
                  pl.BlockSpec((ts, d), row),
                  pl.BlockSpec(lcw.shape, full),
                  pl.BlockSpec(lcb.shape, full),
                  pl.BlockSpec(wg.shape, full),
                  pl.BlockSpec(bg.shape, full),
                  pl.BlockSpec(lam.shape, full),
                  pl.BlockSpec(scw.shape, full),
                  pl.BlockSpec(gain.shape, full),
                  pl.BlockSpec(wout.shape, full),
                  pl.BlockSpec(lng.shape, full),
                  pl.BlockSpec(lnb.shape, full)],
        out_specs=pl.BlockSpec((ts, d), row),
        out_shape=jax.ShapeDtypeStruct((batch * seq, d), F32),
        scratch_shapes=[pltpu.VMEM((1, g), F32),
                        pltpu.VMEM((SUBLANES, g), F32),
                        pltpu.VMEM((SUBLANES, g), F32)],
        compiler_params=pltpu.CompilerParams(dimension_semantics=("arbitrary", "arbitrary"),
                                             vmem_limit_bytes=VMEM_LIMIT),
    )(rest, yatt, x, lcw, lcb, wg, bg, lam, scw, gain, wout, lng, lnb)


def _xattn_kernel(x_ref, kv_ref, wq_ref, wo_ref, lng_ref, lnb_ref, o_ref, *, n_heads):
    x = x_ref[...]
    d = x.shape[1]
    dh = d // n_heads
    q = (_dot(x.astype(_MXU_DTYPE), wq_ref[...]) * (dh ** -0.5)).astype(_MXU_DTYPE)
    outs = []
    for h in range(n_heads):
        s = _dot_nt(q[:, h * dh:(h + 1) * dh], kv_ref[:, h * dh:(h + 1) * dh])
        p = jnp.exp(s - jnp.max(s, axis=-1, keepdims=True))
        p = p / jnp.sum(p, axis=-1, keepdims=True)
        outs.append(_dot(p.astype(_MXU_DTYPE), kv_ref[:, d + h * dh:d + (h + 1) * dh]).astype(_MXU_DTYPE))
    xat = _dot(jnp.concatenate(outs, axis=1), wo_ref[...])
    o_ref[...] = _layer_norm(DN_ALPHA * x + xat, lng_ref[...], lnb_ref[...])


def _xattn(x, kv, wq, wo, lng, lnb, batch, seq, n_mem, ts=512):
    d = x.shape[1]
    assert seq % ts == 0
    nt = seq // ts
    dh = d // XATTN_HEADS
    assert (dh & (dh - 1)) == 0 and (dh.bit_length() - 1) % 2 == 0, "scale must be a power of two"
    row = lambda b, i: (b * nt + i, 0)
    full = lambda b, i: (0, 0)
    return pl.pallas_call(
        functools.partial(_xattn_kernel, n_heads=XATTN_HEADS),
        grid=(batch, nt),
        in_specs=[pl.BlockSpec((ts, d), row),
                  pl.BlockSpec((n_mem, 2 * d), lambda b, i: (b, 0)),
                  pl.BlockSpec(wq.shape, full),
                  pl.BlockSpec(wo.shape, full),
                  pl.BlockSpec(lng.shape, full),
                  pl.BlockSpec(lnb.shape, full)],
        out_specs=pl.BlockSpec((ts, d), row),
        out_shape=jax.ShapeDtypeStruct((batch * seq, d), F32),
        compiler_params=pltpu.CompilerParams(dimension_semantics=("arbitrary", "arbitrary"),
                                             vmem_limit_bytes=VMEM_LIMIT),
    )(x, kv, wq, wo, lng, lnb)


def _block_diag(w):
    h, d, _ = w.shape
    eye = jnp.eye(h, dtype=w.dtype)
    return (eye[:, None, :, None] * w[:, :, None, :]).reshape(h * d, h * d)


def kernel(x, mem, w_in, lru_conv_w, lru_conv_b, lru_w_a, lru_b_a, lru_w_x, lru_b_x, lru_lambda, sc_conv_w,
           group_gain, w_out, ln1_g, ln1_b, xq_w, xk_w, xv_w, xo_w, ln2_g, ln2_b):
    batch, seq, d = x.shape
    n_mem = mem.shape[1]
    g = GROUP_WIDTH
    xf = x.reshape(batch * seq, d)
    memf = mem.reshape(batch * n_mem, d)
    row = lambda v: v.reshape(1, -1)
    tm = 512
    for l in range(w_in.shape[0]):
        w_in_l = w_in[l].astype(_MXU_DTYPE)
        qt, k, vt = _qkv_proj(xf, w_in_l[:, :g].T, w_in_l[:, g:2 * g], w_in_l[:, 2 * g:3 * g].T, tm, MOBA_BLOCK,
                              ATT_HEAD_DIM ** -0.5 * math.log2(math.e))
        rest = _matmul(xf, w_in_l[:, 3 * g:], F32, tm)
        yatt = _moba(qt, k, vt, batch, seq)
        wg = jnp.concatenate([_block_diag(lru_w_a[l]), _block_diag(lru_w_x[l])], axis=1).astype(_MXU_DTYPE)
        bg = jnp.concatenate([lru_b_a[l], lru_b_x[l]]).reshape(1, -1)
        x1 = _mix(rest, yatt, xf, lru_conv_w[l], row(lru_conv_b[l]), wg, bg, row(lru_lambda[l]),
                  sc_conv_w[l], group_gain[l], w_out[l].astype(_MXU_DTYPE), row(ln1_g[l]), row(ln1_b[l]),
                  batch, seq)
        wkv = jnp.concatenate([xk_w[l], xv_w[l]], axis=1).astype(_MXU_DTYPE)
        kv = _matmul(memf, wkv, _MXU_DTYPE, n_mem)
        xf = _xattn(x1, kv, xq_w[l].astype(_MXU_DTYPE), xo_w[l].astype(_MXU_DTYPE), row(ln2_g[l]),
                    row(ln2_b[l]), batch, seq, n_mem)
    return xf.reshape(batch, seq, d)
```

```python
import functools
import math

import jax
import jax.numpy as jnp
from jax import lax
from jax.experimental import pallas as pl
from jax.experimental.pallas import tpu as pltpu

F32 = jnp.float32
_MXU_DTYPE = jnp.bfloat16

GROUP_WIDTH = 512
ATT_HEAD_DIM = 64
MOBA_BLOCK = 256
MOBA_TOPK = 3
LRU_HEADS = 8
LRU_C = 8.0
XATTN_HEADS = 4
DEPTH = 2
DN_ALPHA = (2.0 * DEPTH) ** 0.25
LN_EPS = 1e-5
RMS_EPS = 1e-6

LANES = 128
SUBLANES = 8
PACKED_ROWS = 16
VMEM_LIMIT = 48 * 1024 * 1024
MASK_BIAS = -1e30
MOBA_UNROLL = 4


def _dot(a, b):
    return jnp.dot(a, b, preferred_element_type=F32)


def _dot_nt(a, b):
    return lax.dot_general(a, b, (((1,), (1,)), ((), ())), preferred_element_type=F32)


def _sigmoid(x):
    return 0.5 * jnp.tanh(0.5 * x) + 0.5


def _matmul_kernel(x_ref, w_ref, o_ref, *, n_chunk):
    xb = x_ref[...].astype(_MXU_DTYPE)
    n = w_ref.shape[1]
    for c in range(0, n, n_chunk):
        o_ref[:, c:c + n_chunk] = _dot(xb, w_ref[:, c:c + n_chunk]).astype(o_ref.dtype)


def _matmul(x, w, out_dtype, tm, n_chunk=512):
    m, k = x.shape
    n = w.shape[1]
    assert m % tm == 0 and n % n_chunk == 0
    return pl.pallas_call(
        functools.partial(_matmul_kernel, n_chunk=n_chunk),
        grid=(m // tm,),
        in_specs=[pl.BlockSpec((tm, k), lambda i: (i, 0)),
                  pl.BlockSpec((k, n), lambda i: (0, 0))],
        out_specs=pl.BlockSpec((tm, n), lambda i: (i, 0)),
        out_shape=jax.ShapeDtypeStruct((m, n), out_dtype),
        compiler_params=pltpu.CompilerParams(dimension_semantics=("arbitrary",),
                                             vmem_limit_bytes=VMEM_LIMIT),
    )(x, w)


def _qkv_kernel(x_ref, wqt_ref, wk_ref, wvt_ref, qt_ref, k_ref, vt_ref, *, blk, q_scale):
    xb = x_ref[...].astype(_MXU_DTYPE)
    k_ref[...] = _dot(xb, wk_ref[...]).astype(k_ref.dtype)
    for c in range(x_ref.shape[0] // blk):
        xc = xb[c * blk:(c + 1) * blk, :]
        qt_ref[c] = (_dot_nt(wqt_ref[...], xc) * q_scale).astype(qt_ref.dtype)
        vt_ref[c] = _dot_nt(wvt_ref[...], xc).astype(vt_ref.dtype)


def _qkv_proj(x, wqt, wk, wvt, tm, blk, q_scale):
    m, k = x.shape
    g = wk.shape[1]
    assert m % tm == 0 and tm % blk == 0
    full = lambda i: (0, 0)
    slab = jax.ShapeDtypeStruct((m // blk, g, blk), _MXU_DTYPE)
    return pl.pallas_call(
        functools.partial(_qkv_kernel, blk=blk, q_scale=q_scale),
        grid=(m // tm,),
        in_specs=[pl.BlockSpec((tm, k), lambda i: (i, 0)),
                  pl.BlockSpec(wqt.shape, full),
                  pl.BlockSpec(wk.shape, full),
                  pl.BlockSpec(wvt.shape, full)],
        out_specs=[pl.BlockSpec((tm // blk, g, blk), lambda i: (i, 0, 0)),
                   pl.BlockSpec((tm, g), lambda i: (i, 0)),
                   pl.BlockSpec((tm // blk, g, blk), lambda i: (i, 0, 0))],
        out_shape=[slab, jax.ShapeDtypeStruct((m, g), _MXU_DTYPE), slab],
        compiler_params=pltpu.CompilerParams(dimension_semantics=("arbitrary",),
                                             vmem_limit_bytes=VMEM_LIMIT),
    )(x, wqt, wk, wvt)


def _moba_kernel(qt_ref, k_ref, vt_ref, o_ref, kmean_ref, oh_ref, vat_ref, qat_ref, m_ref, acc_ref, s_ref,
                 *, blk, dh, n_blk, unroll):
    qi = pl.program_id(2)
    n_ones = PACKED_ROWS

    @pl.when(qi == 0)
    def _():
        kmean_ref[...] = jnp.zeros_like(kmean_ref)
        lane = lax.broadcasted_iota(jnp.int32, (blk, LANES), 1)
        ones = jnp.ones((n_ones, blk), vat_ref.dtype)

        def prep_body(j, c):
            sl = pl.ds(pl.multiple_of(j * blk, blk), blk)
            kb = k_ref[sl, :].astype(F32)
            kmean_ref[pl.ds(j, 1), :] = jnp.sum(kb, axis=0, keepdims=True) * (1.0 / blk)
            oh_ref[sl, :] = jnp.where(lane == j, 1.0, 0.0).astype(oh_ref.dtype)
            for h in range(2):
                vat_ref[h, j] = jnp.concatenate([vt_ref[j, h * dh:(h + 1) * dh, :], ones], axis=0)
            return c

        lax.fori_loop(0, n_blk, prep_body, 0)

    qt = qt_ref[0]
    frow = lax.broadcasted_iota(jnp.int32, qt.shape, 0)
    zero = jnp.zeros_like(qt)

    km = kmean_ref[...]
    km_hi = km.astype(_MXU_DTYPE)
    km_lo = (km - km_hi.astype(F32)).astype(_MXU_DTYPE)
    brow = lax.broadcasted_iota(jnp.int32, (n_blk, blk), 0).astype(F32)
    neg_inf = jnp.asarray(-jnp.inf, F32)
    own = pl.ds(pl.multiple_of(qi * blk, blk), blk)
    krow = lax.broadcasted_iota(jnp.int32, (blk, blk), 0)
    qcol = lax.broadcasted_iota(jnp.int32, (blk, blk), 1)

    for h in range(2):
        qth = jnp.where((frow >= h * dh) & (frow < (h + 1) * dh), qt, zero)
        g = (_dot(km_hi, qth) + _dot(km_lo, qth))[:n_blk]
        g = jnp.where(brow < qi.astype(F32), g, neg_inf)
        sel = jnp.zeros(g.shape, jnp.bool_)
        for _ in range(MOBA_TOPK):
            mx = jnp.max(g, axis=0, keepdims=True)
            idx = jnp.min(jnp.where(g == mx, brow, float(n_blk)), axis=0, keepdims=True)
            pick = (brow == idx) & (mx > neg_inf)
            sel = sel | pick
            g = jnp.where(pick, neg_inf, g)
        bias = jnp.where(sel, 0.0, MASK_BIAS).astype(_MXU_DTYPE)
        qat_ref[h] = jnp.concatenate(
            [qth, bias, jnp.zeros((LANES - n_blk, blk), _MXU_DTYPE)], axis=0)

        s = _dot(k_ref[own, :], qth)
        s = jnp.where(krow <= qcol, s, neg_inf)
        m0 = jnp.max(s, axis=0, keepdims=True)
        m_ref[h] = m0
        acc_ref[h] = _dot(vat_ref[h, qi], jnp.exp2(s - m0).astype(_MXU_DTYPE))

    def scores(gi, u):
        j = gi * unroll + u // 2
        sl = pl.ds(pl.multiple_of(j * blk, blk), blk)
        k_aug = jnp.concatenate([k_ref[sl, :], oh_ref[sl, :]], axis=1)
        return _dot(k_aug, qat_ref[u % 2])

    for u in range(2 * unroll):
        s_ref[u] = scores(0, u)

    def group_step(gi, look_ahead):
        for u in range(2 * unroll):
            h = u % 2
            m_old = m_ref[h]
            m_new = jnp.maximum(m_old, jnp.max(s_ref[u], axis=0, keepdims=True))
            p = jnp.exp2(s_ref[u] - m_new).astype(_MXU_DTYPE)
            if look_ahead:
                s_ref[u] = scores(gi + 1, u)
            m_ref[h] = m_new
            acc_ref[h] = jnp.exp2(m_old - m_new) * acc_ref[h] + _dot(vat_ref[h, gi * unroll + u // 2], p)

    n_groups = (qi + unroll - 1) // unroll
    n_ahead = jnp.maximum(n_groups - 1, 0)

    def pair_body(pi, c):
        group_step(2 * pi, True)
        group_step(2 * pi + 1, True)
        return c

    lax.fori_loop(0, n_ahead // 2, pair_body, 0)

    @pl.when(n_ahead % 2 == 1)
    def _():
        group_step(n_ahead - 1, True)

    @pl.when(n_groups > 0)
    def _():
        group_step(n_groups - 1, False)

    outs = []
    for h in range(2):
        acc = acc_ref[h]
        outs.append(acc[:dh] / acc[dh:dh + 1])
    o_ref[...] = jnp.concatenate(outs, axis=0).T.astype(o_ref.dtype)


def _moba(qt, k, vt, batch, seq):
    blk, dh, unroll = MOBA_BLOCK, ATT_HEAD_DIM, MOBA_UNROLL
    assert seq % blk == 0 and 2 * dh == LANES
    n_blk = seq // blk
    assert n_blk <= LANES and n_blk % unroll == 0
    n_pair = GROUP_WIDTH // LANES
    return pl.pallas_call(
        functools.partial(_moba_kernel, blk=blk, dh=dh, n_blk=n_blk, unroll=unroll),
        grid=(batch, n_pair, n_blk),
        in_specs=[pl.BlockSpec((1, LANES, blk), lambda b, h, i: (b * n_blk + i, h, 0)),
                  pl.BlockSpec((seq, LANES), lambda b, h, i: (b, h)),
                  pl.BlockSpec((n_blk, LANES, blk), lambda b, h, i: (b, h, 0))],
        out_specs=pl.BlockSpec((blk, LANES), lambda b, h, i: (b * n_blk + i, h)),
        out_shape=jax.ShapeDtypeStruct((batch * seq, GROUP_WIDTH), F32),
        scratch_shapes=[pltpu.VMEM((LANES, LANES), F32),
                        pltpu.VMEM((seq, LANES), _MXU_DTYPE),
                        pltpu.VMEM((2, n_blk, dh + PACKED_ROWS, blk), _MXU_DTYPE),
                        pltpu.VMEM((2, 2 * LANES, blk), _MXU_DTYPE),
                        pltpu.VMEM((2, 1, blk), F32),
                        pltpu.VMEM((2, dh + PACKED_ROWS, blk), F32),
                        pltpu.VMEM((2 * unroll, blk, blk), F32)],
        compiler_params=pltpu.CompilerParams(dimension_semantics=("arbitrary", "arbitrary", "arbitrary"),
                                             vmem_limit_bytes=VMEM_LIMIT),
    )(qt, k, vt)


def _shift_rows(x, tail, s):
    if s == 0:
        return x
    xs = pltpu.roll(x, s, axis=0)
    ts = pltpu.roll(tail, s, axis=0)
    rid = lax.broadcasted_iota(jnp.int32, tail.shape, 0)
    head = jnp.where(rid < s, ts, xs[:SUBLANES])
    return jnp.concatenate([head, xs[SUBLANES:]], axis=0)


def _causal_dwconv(x, tail, w_ref):
    kk = w_ref.shape[0]
    y = None
    for k in range(kk):
        term = w_ref[k:k + 1, :] * _shift_rows(x, tail, kk - 1 - k)
        y = term if y is None else y + term
    return y


def _rms_gate(y, gain, gate):
    r = lax.rsqrt(jnp.mean(y * y, axis=-1, keepdims=True) + RMS_EPS)
    return y * r * gain * (gate * _sigmoid(gate))


def _layer_norm(x, g, b):
    mu = jnp.mean(x, axis=-1, keepdims=True)
    xc = x - mu
    var = jnp.mean(xc * xc, axis=-1, keepdims=True)
    return xc * lax.rsqrt(var + LN_EPS) * g + b


def _mix_kernel(rest_ref, yatt_ref, x_ref, lcw_ref, lcb_ref, wg_ref, bg_ref, lam_ref, scw_ref, gain_ref,
                wout_ref, lng_ref, lnb_ref, o_ref, h_ref, tail_lru_ref, tail_sc_ref, *, ts):
    g = GROUP_WIDTH

    @pl.when(pl.program_id(1) == 0)
    def _():
        h_ref[...] = jnp.zeros_like(h_ref)
        tail_lru_ref[...] = jnp.zeros_like(tail_lru_ref)
        tail_sc_ref[...] = jnp.zeros_like(tail_sc_ref)

    x_lru = rest_ref[:, g:2 * g]
    xl = _causal_dwconv(x_lru, tail_lru_ref[...], lcw_ref) + lcb_ref[...]
    tail_lru_ref[...] = x_lru[ts - SUBLANES:, :]
    pre = _dot(xl.astype(_MXU_DTYPE), wg_ref[...]) + bg_ref[...]
    r = _sigmoid(pre[:, :g])
    ig = _sigmoid(pre[:, g:])
    neg_lam = -lam_ref[...]
    softplus = jnp.maximum(neg_lam, 0.0) + jnp.log1p(jnp.exp(-jnp.abs(neg_lam)))
    log_a = -LRU_C * r * softplus
    a = jnp.exp(log_a)
    y2 = -jnp.tanh(log_a) * (a * a + 1.0)
    u = jnp.where(y2 > 0.0, y2 * lax.rsqrt(y2), 0.0) * (ig * xl)
    rid = lax.broadcasted_iota(jnp.int32, a.shape, 0)
    d = 1
    while d < ts:
        if d < SUBLANES:
            keep = rid >= d
            u = jnp.where(keep, a * pltpu.roll(u, d, axis=0) + u, u)
            a = jnp.where(keep, a * pltpu.roll(a, d, axis=0), a)
        else:
            u = jnp.concatenate([u[:d], a[d:] * u[:ts - d] + u[d:]], axis=0)
            a = jnp.concatenate([a[:d], a[d:] * a[:ts - d]], axis=0)
        d *= 2
    y_lru = a * h_ref[...] + u
    h_ref[...] = y_lru[ts - 1:, :]

    z = rest_ref[:, 4 * g:5 * g] * rest_ref[:, 5 * g:6 * g]
    y_sc = rest_ref[:, 3 * g:4 * g] * _causal_dwconv(z, tail_sc_ref[...], scw_ref)
    tail_sc_ref[...] = z[ts - SUBLANES:, :]

    y = jnp.concatenate([
        _rms_gate(yatt_ref[...].astype(F32), gain_ref[0:1, :], rest_ref[:, 0:g]).astype(_MXU_DTYPE),
        _rms_gate(y_lru, gain_ref[1:2, :], rest_ref[:, 2 * g:3 * g]).astype(_MXU_DTYPE),
        _rms_gate(y_sc, gain_ref[2:3, :], rest_ref[:, 6 * g:7 * g]).astype(_MXU_DTYPE)], axis=1)
    mix = _dot(y, wout_ref[...])
    o_ref[...] = _layer_norm(DN_ALPHA * x_ref[...] + mix, lng_ref[...], lnb_ref[...])


def _mix(rest, yatt, x, lcw, lcb, wg, bg, lam, scw, gain, wout, lng, lnb, batch, seq, ts=256):
    d = x.shape[1]
    g = GROUP_WIDTH
    assert seq % ts == 0 and ts % SUBLANES == 0
    nt = seq // ts
    row = lambda b, i: (b * nt + i, 0)
    full = lambda b, i: (0, 0)
    return pl.pallas_call(
        functools.partial(_mix_kernel, ts=ts),
        grid=(batch, nt),
        in_specs=[pl.BlockSpec((ts, 7 * g), row),
                  pl.BlockSpec((ts, g), row),
                  pl.BlockSpec((ts, d), row),
                  pl.BlockSpec(lcw.shape, full),
                  pl.BlockSpec(lcb.shape, full),
                  pl.BlockSpec(wg.shape, full),
                  pl.BlockSpec(bg.shape, full),
                  pl.BlockSpec(lam.shape, full),
                  pl.BlockSpec(scw.shape, full),
                  pl.BlockSpec(gain.shape, full),
                  pl.BlockSpec(wout.shape, full),
                  pl.BlockSpec(lng.shape, full),
                  pl.BlockSpec(lnb.shape, full)],
        out_specs=pl.BlockSpec((ts, d), row),
        out_shape=jax.ShapeDtypeStruct((batch * seq, d), F32),
        scratch_shapes=[pltpu.VMEM((1, g), F32),
                        pltpu.VMEM((SUBLANES, g), F32),
                        pltpu.VMEM((SUBLANES, g), F32)],
        compiler_params=pltpu.CompilerParams(dimension_semantics=("arbitrary", "arbitrary"),
                                             vmem_limit_bytes=VMEM_LIMIT),
    )(rest, yatt, x, lcw, lcb, wg, bg, lam, scw, gain, wout, lng, lnb)


def _xattn_kernel(x_ref, kv_ref, wq_ref, wo_ref, lng_ref, lnb_ref, o_ref, *, n_heads):
    x = x_ref[...]
    d = x.shape[1]
    dh = d // n_heads
    q = (_dot(x.astype(_MXU_DTYPE), wq_ref[...]) * (dh ** -0.5)).astype(_MXU_DTYPE)
    outs = []
    for h in range(n_heads):
        s = _dot_nt(q[:, h * dh:(h + 1) * dh], kv_ref[:, h * dh:(h + 1) * dh])
        p = jnp.exp(s - jnp.max(s, axis=-1, keepdims=True))
        inv_l = 1.0 / jnp.sum(p, axis=-1, keepdims=True)
        o = _dot(p.astype(_MXU_DTYPE), kv_ref[:, d + h * dh:d + (h + 1) * dh]) * inv_l
        outs.append(o.astype(_MXU_DTYPE))
    xat = _dot(jnp.concatenate(outs, axis=1), wo_ref[...])
    o_ref[...] = _layer_norm(DN_ALPHA * x + xat, lng_ref[...], lnb_ref[...])


def _xattn(x, kv, wq, wo, lng, lnb, batch, seq, n_mem, ts=512):
    d = x.shape[1]
    assert seq % ts == 0
    nt = seq // ts
    dh = d // XATTN_HEADS
    assert (dh & (dh - 1)) == 0 and (dh.bit_length() - 1) % 2 == 0, "scale must be a power of two"
    row = lambda b, i: (b * nt + i, 0)
    full = lambda b, i: (0, 0)
    return pl.pallas_call(
        functools.partial(_xattn_kernel, n_heads=XATTN_HEADS),
        grid=(batch, nt),
        in_specs=[pl.BlockSpec((ts, d), row),
                  pl.BlockSpec((n_mem, 2 * d), lambda b, i: (b, 0)),
                  pl.BlockSpec(wq.shape, full),
                  pl.BlockSpec(wo.shape, full),
                  pl.BlockSpec(lng.shape, full),
                  pl.BlockSpec(lnb.shape, full)],
        out_specs=pl.BlockSpec((ts, d), row),
        out_shape=jax.ShapeDtypeStruct((batch * seq, d), F32),
        compiler_params=pltpu.CompilerParams(dimension_semantics=("arbitrary", "arbitrary"),
                                             vmem_limit_bytes=VMEM_LIMIT),
    )(x, kv, wq, wo, lng, lnb)


def _block_diag(w):
    h, d, _ = w.shape
    eye = jnp.eye(h, dtype=w.dtype)
    return (eye[:, None, :, None] * w[:, :, None, :]).reshape(h * d, h * d)


def kernel(x, mem, w_in, lru_conv_w, lru_conv_b, lru_w_a, lru_b_a, lru_w_x, lru_b_x, lru_lambda, sc_conv_w,
           group_gain, w_out, ln1_g, ln1_b, xq_w, xk_w, xv_w, xo_w, ln2_g, ln2_b):
    batch, seq, d = x.shape
    n_mem = mem.shape[1]
    g = GROUP_WIDTH
    xf = x.reshape(batch * seq, d)
    memf = mem.reshape(batch * n_mem, d)
    row = lambda v: v.reshape(1, -1)
    tm = 512
    for l in range(w_in.shape[0]):
        w_in_l = w_in[l].astype(_MXU_DTYPE)
        qt, k, vt = _qkv_proj(xf, w_in_l[:, :g].T, w_in_l[:, g:2 * g], w_in_l[:, 2 * g:3 * g].T, tm, MOBA_BLOCK,
                              ATT_HEAD_DIM ** -0.5 * math.log2(math.e))
        rest = _matmul(xf, w_in_l[:, 3 * g:], F32, tm)
        yatt = _moba(qt, k, vt, batch, seq)
        wg = jnp.concatenate([_block_diag(lru_w_a[l]), _block_diag(lru_w_x[l])], axis=1).astype(_MXU_DTYPE)
        bg = jnp.concatenate([lru_b_a[l], lru_b_x[l]]).reshape(1, -1)
        x1 = _mix(rest, yatt, xf, lru_conv_w[l], row(lru_conv_b[l]), wg, bg, row(lru_lambda[l]),
                  sc_conv_w[l], group_gain[l], w_out[l].astype(_MXU_DTYPE), row(ln1_g[l]), row(ln1_b[l]),
                  batch, seq)
        wkv = jnp.concatenate([xk_w[l], xv_w[l]], axis=1).astype(_MXU_DTYPE)
        kv = _matmul(memf, wkv, _MXU_DTYPE, n_mem)
        xf = _xattn(x1, kv, xq_w[l].astype(_MXU_DTYPE), xo_w[l].astype(_MXU_DTYPE), row(ln2_g[l]),
                    row(ln2_b[l]), batch, seq, n_mem)
    return xf.reshape(batch, seq, d)
```

```python
import functools
import math

import jax
import jax.numpy as jnp
from jax import lax
from jax.experimental import pallas as pl
from jax.experimental.pallas import tpu as pltpu

F32 = jnp.float32
_MXU_DTYPE = jnp.bfloat16

GROUP_WIDTH = 512
ATT_HEAD_DIM = 64
MOBA_BLOCK = 256
MOBA_TOPK = 3
LRU_HEADS = 8
LRU_C = 8.0
XATTN_HEADS = 4
DEPTH = 2
DN_ALPHA = (2.0 * DEPTH) ** 0.25
LN_EPS = 1e-5
RMS_EPS = 1e-6

LANES = 128
SUBLANES = 8
PACKED_ROWS = 16
VMEM_LIMIT = 48 * 1024 * 1024
MASK_BIAS = -1e30
MOBA_UNROLL = 4


def _dot(a, b):
    return jnp.dot(a, b, preferred_element_type=F32)


def _dot_nt(a, b):
    return lax.dot_general(a, b, (((1,), (1,)), ((), ())), preferred_element_type=F32)


def _sigmoid(x):
    return 0.5 * jnp.tanh(0.5 * x) + 0.5


def _matmul_kernel(x_ref, w_ref, o_ref, *, n_chunk):
    xb = x_ref[...].astype(_MXU_DTYPE)
    n = w_ref.shape[1]
    for c in range(0, n, n_chunk):
        o_ref[:, c:c + n_chunk] = _dot(xb, w_ref[:, c:c + n_chunk]).astype(o_ref.dtype)


def _matmul(x, w, out_dtype, tm, n_chunk=512):
    m, k = x.shape
    n = w.shape[1]
    assert m % tm == 0 and n % n_chunk == 0
    return pl.pallas_call(
        functools.partial(_matmul_kernel, n_chunk=n_chunk),
        grid=(m // tm,),
        in_specs=[pl.BlockSpec((tm, k), lambda i: (i, 0)),
                  pl.BlockSpec((k, n), lambda i: (0, 0))],
        out_specs=pl.BlockSpec((tm, n), lambda i: (i, 0)),
        out_shape=jax.ShapeDtypeStruct((m, n), out_dtype),
        compiler_params=pltpu.CompilerParams(dimension_semantics=("arbitrary",),
                                             vmem_limit_bytes=VMEM_LIMIT),
    )(x, w)


def _qkv_kernel(x_ref, wqt_ref, wk_ref, wvt_ref, qt_ref, k_ref, vt_ref, *, blk, q_scale):
    xb = x_ref[...].astype(_MXU_DTYPE)
    k_ref[...] = _dot(xb, wk_ref[...]).astype(k_ref.dtype)
    for c in range(x_ref.shape[0] // blk):
        xc = xb[c * blk:(c + 1) * blk, :]
        qt_ref[c] = (_dot_nt(wqt_ref[...], xc) * q_scale).astype(qt_ref.dtype)
        vt_ref[c] = _dot_nt(wvt_ref[...], xc).astype(vt_ref.dtype)


def _qkv_proj(x, wqt, wk, wvt, tm, blk, q_scale):
    m, k = x.shape
    g = wk.shape[1]
    assert m % tm == 0 and tm % blk == 0
    full = lambda i: (0, 0)
    slab = jax.ShapeDtypeStruct((m // blk, g, blk), _MXU_DTYPE)
    return pl.pallas_call(
        functools.partial(_qkv_kernel, blk=blk, q_scale=q_scale),
        grid=(m // tm,),
        in_specs=[pl.BlockSpec((tm, k), lambda i: (i, 0)),
                  pl.BlockSpec(wqt.shape, full),
                  pl.BlockSpec(wk.shape, full),
                  pl.BlockSpec(wvt.shape, full)],
        out_specs=[pl.BlockSpec((tm // blk, g, blk), lambda i: (i, 0, 0)),
                   pl.BlockSpec((tm, g), lambda i: (i, 0)),
                   pl.BlockSpec((tm // blk, g, blk), lambda i: (i, 0, 0))],
        out_shape=[slab, jax.ShapeDtypeStruct((m, g), _MXU_DTYPE), slab],
        compiler_params=pltpu.CompilerParams(dimension_semantics=("arbitrary",),
                                             vmem_limit_bytes=VMEM_LIMIT),
    )(x, wqt, wk, wvt)


def _moba_kernel(qt_ref, k_ref, vt_ref, o_ref, kmean_ref, oh_ref, vat_ref, qat_ref, m_ref, acc_ref, s_ref,
                 *, blk, dh, n_blk, unroll):
    qi = pl.program_id(2)
    n_ones = PACKED_ROWS

    @pl.when(qi == 0)
    def _():
        kmean_ref[...] = jnp.zeros_like(kmean_ref)
        lane = lax.broadcasted_iota(jnp.int32, (blk, LANES), 1)
        ones = jnp.ones((n_ones, blk), vat_ref.dtype)

        def prep_body(j, c):
            sl = pl.ds(pl.multiple_of(j * blk, blk), blk)
            kb = k_ref[sl, :].astype(F32)
            kmean_ref[pl.ds(j, 1), :] = jnp.sum(kb, axis=0, keepdims=True) * (1.0 / blk)
            oh_ref[sl, :] = jnp.where(lane == j, 1.0, 0.0).astype(oh_ref.dtype)
            for h in range(2):
                vat_ref[h, j] = jnp.concatenate([vt_ref[j, h * dh:(h + 1) * dh, :], ones], axis=0)
            return c

        lax.fori_loop(0, n_blk, prep_body, 0)

    qt = qt_ref[0]
    frow = lax.broadcasted_iota(jnp.int32, qt.shape, 0)
    zero = jnp.zeros_like(qt)

    km = kmean_ref[...]
    km_hi = km.astype(_MXU_DTYPE)
    km_lo = (km - km_hi.astype(F32)).astype(_MXU_DTYPE)
    brow = lax.broadcasted_iota(jnp.int32, (n_blk, blk), 0).astype(F32)
    neg_inf = jnp.asarray(-jnp.inf, F32)
    own = pl.ds(pl.multiple_of(qi * blk, blk), blk)
    krow = lax.broadcasted_iota(jnp.int32, (blk, blk), 0)
    qcol = lax.broadcasted_iota(jnp.int32, (blk, blk), 1)

    for h in range(2):
        qth = jnp.where((frow >= h * dh) & (frow < (h + 1) * dh), qt, zero)
        g = (_dot(km_hi, qth) + _dot(km_lo, qth))[:n_blk]
        g = jnp.where(brow < qi.astype(F32), g, neg_inf)
        sel = jnp.zeros(g.shape, jnp.bool_)
        for _ in range(MOBA_TOPK):
            mx = jnp.max(g, axis=0, keepdims=True)
            idx = jnp.min(jnp.where(g == mx, brow, float(n_blk)), axis=0, keepdims=True)
            pick = (brow == idx) & (mx > neg_inf)
            sel = sel | pick
            g = jnp.where(pick, neg_inf, g)
        bias = jnp.where(sel, 0.0, MASK_BIAS).astype(_MXU_DTYPE)
        qat_ref[h] = jnp.concatenate(
            [qth, bias, jnp.zeros((LANES - n_blk, blk), _MXU_DTYPE)], axis=0)

        s = _dot(k_ref[own, :], qth)
        s = jnp.where(krow <= qcol, s, neg_inf)
        m0 = jnp.max(s, axis=0, keepdims=True)
        m_ref[h] = m0
        acc_ref[h] = _dot(vat_ref[h, qi], jnp.exp2(s - m0).astype(_MXU_DTYPE))

    def scores(gi, u):
        j = gi * unroll + u // 2
        sl = pl.ds(pl.multiple_of(j * blk, blk), blk)
        k_aug = jnp.concatenate([k_ref[sl, :], oh_ref[sl, :]], axis=1)
        return _dot(k_aug, qat_ref[u % 2])

    for u in range(2 * unroll):
        s_ref[u] = scores(0, u)

    def group_step(gi, look_ahead):
        for u in range(2 * unroll):
            h = u % 2
            m_old = m_ref[h]
            m_new = jnp.maximum(m_old, jnp.max(s_ref[u], axis=0, keepdims=True))
            p = jnp.exp2(s_ref[u] - m_new).astype(_MXU_DTYPE)
            if look_ahead:
                s_ref[u] = scores(gi + 1, u)
            m_ref[h] = m_new
            acc_ref[h] = jnp.exp2(m_old - m_new) * acc_ref[h] + _dot(vat_ref[h, gi * unroll + u // 2], p)

    n_groups = (qi + unroll - 1) // unroll
    n_ahead = jnp.maximum(n_groups - 1, 0)

    def quad_body(pi, c):
        for t in range(4):
            group_step(4 * pi + t, True)
        return c

    n_quads = n_ahead // 4
    lax.fori_loop(0, n_quads, quad_body, 0)

    @pl.when(n_ahead - 4 * n_quads >= 2)
    def _():
        group_step(4 * n_quads, True)
        group_step(4 * n_quads + 1, True)

    @pl.when(n_ahead % 2 == 1)
    def _():
        group_step(n_ahead - 1, True)

    @pl.when(n_groups > 0)
    def _():
        group_step(n_groups - 1, False)

    outs = []
    for h in range(2):
        acc = acc_ref[h]
        outs.append(acc[:dh] / acc[dh:dh + 1])
    o_ref[...] = jnp.concatenate(outs, axis=0).T.astype(o_ref.dtype)


def _moba(qt, k, vt, batch, seq):
    blk, dh, unroll = MOBA_BLOCK, ATT_HEAD_DIM, MOBA_UNROLL
    assert seq % blk == 0 and 2 * dh == LANES
    n_blk = seq // blk
    assert n_blk <= LANES and n_blk % unroll == 0
    n_pair = GROUP_WIDTH // LANES
    return pl.pallas_call(
        functools.partial(_moba_kernel, blk=blk, dh=dh, n_blk=n_blk, unroll=unroll),
        grid=(batch, n_pair, n_blk),
        in_specs=[pl.BlockSpec((1, LANES, blk), lambda b, h, i: (b * n_blk + i, h, 0)),
                  pl.BlockSpec((seq, LANES), lambda b, h, i: (b, h)),
                  pl.BlockSpec((n_blk, LANES, blk), lambda b, h, i: (b, h, 0))],
        out_specs=pl.BlockSpec((blk, LANES), lambda b, h, i: (b * n_blk + i, h)),
        out_shape=jax.ShapeDtypeStruct((batch * seq, GROUP_WIDTH), F32),
        scratch_shapes=[pltpu.VMEM((LANES, LANES), F32),
                        pltpu.VMEM((seq, LANES), _MXU_DTYPE),
                        pltpu.VMEM((2, n_blk, dh + PACKED_ROWS, blk), _MXU_DTYPE),
                        pltpu.VMEM((2, 2 * LANES, blk), _MXU_DTYPE),
                        pltpu.VMEM((2, 1, blk), F32),
                        pltpu.VMEM((2, dh + PACKED_ROWS, blk), F32),
                        pltpu.VMEM((2 * unroll, blk, blk), F32)],
        compiler_params=pltpu.CompilerParams(dimension_semantics=("arbitrary", "arbitrary", "arbitrary"),
                                             vmem_limit_bytes=VMEM_LIMIT),
    )(qt, k, vt)


def _shift_rows(x, tail, s):
    if s == 0:
        return x
    xs = pltpu.roll(x, s, axis=0)
    ts = pltpu.roll(tail, s, axis=0)
    rid = lax.broadcasted_iota(jnp.int32, tail.shape, 0)
    head = jnp.where(rid < s, ts, xs[:SUBLANES])
    return jnp.concatenate([head, xs[SUBLANES:]], axis=0)


def _causal_dwconv(x, tail, w_ref):
    kk = w_ref.shape[0]
    y = None
    for k in range(kk):
        term = w_ref[k:k + 1, :] * _shift_rows(x, tail, kk - 1 - k)
        y = term if y is None else y + term
    return y


def _rms_gate(y, gain, gate):
    r = lax.rsqrt(jnp.mean(y * y, axis=-1, keepdims=True) + RMS_EPS)
    return y * r * gain * (gate * _sigmoid(gate))


def _layer_norm(x, g, b):
    mu = jnp.mean(x, axis=-1, keepdims=True)
    xc = x - mu
    var = jnp.mean(xc * xc, axis=-1, keepdims=True)
    return xc * lax.rsqrt(var + LN_EPS) * g + b


def _mix_kernel(rest_ref, yatt_ref, x_ref, lcw_ref, lcb_ref, wg_ref, bg_ref, lam_ref, scw_ref, gain_ref,
                wout_ref, lng_ref, lnb_ref, o_ref, h_ref, tail_lru_ref, tail_sc_ref, *, ts):
    g = GROUP_WIDTH

    @pl.when(pl.program_id(1) == 0)
    def _():
        h_ref[...] = jnp.zeros_like(h_ref)
        tail_lru_ref[...] = jnp.zeros_like(tail_lru_ref)
        tail_sc_ref[...] = jnp.zeros_like(tail_sc_ref)

    x_lru = rest_ref[:, g:2 * g]
    xl = _causal_dwconv(x_lru, tail_lru_ref[...], lcw_ref) + lcb_ref[...]
    tail_lru_ref[...] = x_lru[ts - SUBLANES:, :]
    pre = _dot(xl.astype(_MXU_DTYPE), wg_ref[...]) + bg_ref[...]
    r = _sigmoid(pre[:, :g])
    ig = _sigmoid(pre[:, g:])
    neg_lam = -lam_ref[...]
    softplus = jnp.maximum(neg_lam, 0.0) + jnp.log1p(jnp.exp(-jnp.abs(neg_lam)))
    log_a = -LRU_C * r * softplus
    a = jnp.exp(log_a)
    y2 = -jnp.tanh(log_a) * (a * a + 1.0)
    u = jnp.where(y2 > 0.0, y2 * lax.rsqrt(y2), 0.0) * (ig * xl)
    rid = lax.broadcasted_iota(jnp.int32, a.shape, 0)
    d = 1
    while d < ts:
        if d < SUBLANES:
            keep = rid >= d
            u = jnp.where(keep, a * pltpu.roll(u, d, axis=0) + u, u)
            a = jnp.where(keep, a * pltpu.roll(a, d, axis=0), a)
        else:
            u = jnp.concatenate([u[:d], a[d:] * u[:ts - d] + u[d:]], axis=0)
            a = jnp.concatenate([a[:d], a[d:] * a[:ts - d]], axis=0)
        d *= 2
    y_lru = a * h_ref[...] + u
    h_ref[...] = y_lru[ts - 1:, :]

    z = rest_ref[:, 4 * g:5 * g] * rest_ref[:, 5 * g:6 * g]
    y_sc = rest_ref[:, 3 * g:4 * g] * _causal_dwconv(z, tail_sc_ref[...], scw_ref)
    tail_sc_ref[...] = z[ts - SUBLANES:, :]

    y = jnp.concatenate([
        _rms_gate(yatt_ref[...].astype(F32), gain_ref[0:1, :], rest_ref[:, 0:g]).astype(_MXU_DTYPE),
        _rms_gate(y_lru, gain_ref[1:2, :], rest_ref[:, 2 * g:3 * g]).astype(_MXU_DTYPE),
        _rms_gate(y_sc, gain_ref[2:3, :], rest_ref[:, 6 * g:7 * g]).astype(_MXU_DTYPE)], axis=1)
    mix = _dot(y, wout_ref[...])
    o_ref[...] = _layer_norm(DN_ALPHA * x_ref[...] + mix, lng_ref[...], lnb_ref[...])


def _mix(rest, yatt, x, lcw, lcb, wg, bg, lam, scw, gain, wout, lng, lnb, batch, seq, ts=256):
    d = x.shape[1]
    g = GROUP_WIDTH
    assert seq % ts == 0 and ts % SUBLANES == 0
    nt = seq // ts
    row = lambda b, i: (b * nt + i, 0)
    full = lambda b, i: (0, 0)
    return pl.pallas_call(
        functools.partial(_mix_kernel, ts=ts),
        grid=(batch, nt),
        in_specs=[pl.BlockSpec((ts, 7 * g), row),
                  pl.BlockSpec((ts, g), row),
                  pl.BlockSpec((ts, d), row),
                  pl.BlockSpec(lcw.shape, full),
                  pl.BlockSpec(lcb.shape, full),
                  pl.BlockSpec(wg.shape, full),
                  pl.BlockSpec(bg.shape, full),
                  pl.BlockSpec(lam.shape, full),
                  pl.BlockSpec(scw.shape, full),
                  pl.BlockSpec(gain.shape, full),
                  pl.BlockSpec(wout.shape, full),
                  pl.BlockSpec(lng.shape, full),
                  pl.BlockSpec(lnb.shape, full)],
        out_specs=pl.BlockSpec((ts, d), row),
        out_shape=jax.ShapeDtypeStruct((batch * seq, d), F32),
        scratch_shapes=[pltpu.VMEM((1, g), F32),
                        pltpu.VMEM((SUBLANES, g), F32),
                        pltpu.VMEM((SUBLANES, g), F32)],
        compiler_params=pltpu.CompilerParams(dimension_semantics=("arbitrary", "arbitrary"),
                                             vmem_limit_bytes=VMEM_LIMIT),
    )(rest, yatt, x, lcw, lcb, wg, bg, lam, scw, gain, wout, lng, lnb)


def _xattn_kernel(x_ref, kv_ref, wq_ref, wo_ref, lng_ref, lnb_ref, o_ref, *, n_heads):
    x = x_ref[...]
    d = x.shape[1]
    dh = d // n_heads
    q = (_dot(x.astype(_MXU_DTYPE), wq_ref[...]) * (dh ** -0.5)).astype(_MXU_DTYPE)
    outs = []
    for h in range(n_heads):
        s = _dot_nt(q[:, h * dh:(h + 1) * dh], kv_ref[:, h * dh:(h + 1) * dh])
        p = jnp.exp(s - jnp.max(s, axis=-1, keepdims=True))
        inv_l = 1.0 / jnp.sum(p, axis=-1, keepdims=True)
        o = _dot(p.astype(_MXU_DTYPE), kv_ref[:, d + h * dh:d + (h + 1) * dh]) * inv_l
        outs.append(o.astype(_MXU_DTYPE))
    xat = _dot(jnp.concatenate(outs, axis=1), wo_ref[...])
    o_ref[...] = _layer_norm(DN_ALPHA * x + xat, lng_ref[...], lnb_ref[...])


def _xattn(x, kv, wq, wo, lng, lnb, batch, seq, n_mem, ts=512):
    d = x.shape[1]
    assert seq % ts == 0
    nt = seq // ts
    dh = d // XATTN_HEADS
    assert (dh & (dh - 1)) == 0 and (dh.bit_length() - 1) % 2 == 0, "scale must be a power of two"
    row = lambda b, i: (b * nt + i, 0)
    full = lambda b, i: (0, 0)
    return pl.pallas_call(
        functools.partial(_xattn_kernel, n_heads=XATTN_HEADS),
        grid=(batch, nt),
        in_specs=[pl.BlockSpec((ts, d), row),
                  pl.BlockSpec((n_mem, 2 * d), lambda b, i: (b, 0)),
                  pl.BlockSpec(wq.shape, full),
                  pl.BlockSpec(wo.shape, full),
                  pl.BlockSpec(lng.shape, full),
                  pl.BlockSpec(lnb.shape, full)],
        out_specs=pl.BlockSpec((ts, d), row),
        out_shape=jax.ShapeDtypeStruct((batch * seq, d), F32),
        compiler_params=pltpu.CompilerParams(dimension_semantics=("arbitrary", "arbitrary"),
                                             vmem_limit_bytes=VMEM_LIMIT),
    )(x, kv, wq, wo, lng, lnb)


def _block_diag(w):
    h, d, _ = w.shape
    eye = jnp.eye(h, dtype=w.dtype)
    return (eye[:, None, :, None] * w[:, :, None, :]).reshape(h * d, h * d)


def kernel(x, mem, w_in, lru_conv_w, lru_conv_b, lru_w_a, lru_b_a, lru_w_x, lru_b_x, lru_lambda, sc_conv_w,
           group_gain, w_out, ln1_g, ln1_b, xq_w, xk_w, xv_w, xo_w, ln2_g, ln2_b):
    batch, seq, d = x.shape
    n_mem = mem.shape[1]
    g = GROUP_WIDTH
    xf = x.reshape(batch * seq, d)
    memf = mem.reshape(batch * n_mem, d)
    row = lambda v: v.reshape(1, -1)
    tm = 512
    for l in range(w_in.shape[0]):
        w_in_l = w_in[l].astype(_MXU_DTYPE)
        qt, k, vt = _qkv_proj(xf, w_in_l[:, :g].T, w_in_l[:, g:2 * g], w_in_l[:, 2 * g:3 * g].T, tm, MOBA_BLOCK,
                              ATT_HEAD_DIM ** -0.5 * math.log2(math.e))
        rest = _matmul(xf, w_in_l[:, 3 * g:], F32, tm)
        yatt = _moba(qt, k, vt, batch, seq)
        wg = jnp.concatenate([_block_diag(lru_w_a[l]), _block_diag(lru_w_x[l])], axis=1).astype(_MXU_DTYPE)
        bg = jnp.concatenate([lru_b_a[l], lru_b_x[l]]).reshape(1, -1)
        x1 = _mix(rest, yatt, xf, lru_conv_w[l], row(lru_conv_b[l]), wg, bg, row(lru_lambda[l]),
                  sc_conv_w[l], group_gain[l], w_out[l].astype(_MXU_DTYPE), row(ln1_g[l]), row(ln1_b[l]),
                  batch, seq)
        wkv = jnp.concatenate([xk_w[l], xv_w[l]], axis=1).astype(_MXU_DTYPE)
        kv = _matmul(memf, wkv, _MXU_DTYPE, n_mem)
        xf = _xattn(x1, kv, xq_w[l].astype(_MXU_DTYPE), xo_w[l].astype(_MXU_DTYPE), row(ln2_g[l]),
                    row(ln2_b[l]), batch, seq, n_mem)
    return xf.reshape(batch, seq, d)
```
